```python
import math
import jax, jax.numpy as jnp
from jax import lax
import numpy as np

D_MODEL = 2048
BATCH = 4
SEQ = 4096
DEPTH = 4

D_MIX = D_MODEL
HEAD_DIM = 128
N_HEADS_A = 8
N_HEADS_B = 8
D_A = N_HEADS_A * HEAD_DIM
D_B = N_HEADS_B * HEAD_DIM
IDX_HEADS = 16
IDX_DIM = 64
TOPK_MAX = 256
Q_BLOCK = 128
RET_CHUNK = 128
N_BUCKETS = 32
MAX_EXACT = 16
MAX_DISTANCE = 128
ROPE_BASE = 10000.0
EPS = 1e-6
IN_SIZES = (D_A, HEAD_DIM, HEAD_DIM, D_A, IDX_HEADS * IDX_DIM, IDX_DIM, IDX_HEADS, D_B, D_B, D_B, D_B)
N_IN = 2 * D_A + 2 * HEAD_DIM + IDX_HEADS * IDX_DIM + IDX_DIM + IDX_HEADS + 4 * D_B

kernel_name = "hymba_dsa_retnet_hybrid"


def rmsnorm(x, g):
    xf = x.astype(jnp.float32)
    y = xf * lax.rsqrt(jnp.mean(xf * xf, axis=-1, keepdims=True) + EPS)
    return (y * g.astype(jnp.float32)).astype(x.dtype)


def head_groupnorm(x, g):
    xf = x.astype(jnp.float32)
    mu = jnp.mean(xf, axis=-1, keepdims=True)
    var = jnp.mean(jnp.square(xf - mu), axis=-1, keepdims=True)
    y = (xf - mu) * lax.rsqrt(var + EPS)
    B, S, H, D = x.shape
    return (y.reshape(B, S, H * D) * g.astype(jnp.float32)).astype(x.dtype)


def rotary(x, pos):
    half = x.shape[-1] // 2
    inv = ROPE_BASE ** (-jnp.arange(half, dtype=jnp.float32) / half)
    ang = pos.astype(jnp.float32)[..., None] * inv
    cos = jnp.cos(ang)[:, :, None, :]
    sin = jnp.sin(ang)[:, :, None, :]
    xf = x.astype(jnp.float32)
    x1, x2 = xf[..., :half], xf[..., half:]
    return jnp.concatenate([x1 * cos - x2 * sin, x1 * sin + x2 * cos], axis=-1).astype(x.dtype)


def t5_bucket(rel):
    rel = jnp.maximum(rel, 0)
    relf = jnp.maximum(rel, 1).astype(jnp.float32)
    large = MAX_EXACT + (jnp.log(relf / MAX_EXACT) / math.log(MAX_DISTANCE / MAX_EXACT)
                         * (N_BUCKETS - MAX_EXACT)).astype(jnp.int32)
    large = jnp.minimum(large, N_BUCKETS - 1)
    return jnp.where(rel < MAX_EXACT, rel, large)


def sparse_attention(q, k, v, q_idx, k_idx, w_idx, pos, rel_bias):
    B, S, H, Dh = q.shape
    n_blk = S // Q_BLOCK
    top_k = min(TOPK_MAX, S // 4)
    scale = HEAD_DIM ** -0.5
    key_pos = jnp.arange(S)

    def block(xs):
        qb, qib, wb, pb, start = xs
        t = start + jnp.arange(Q_BLOCK)
        dots = jnp.einsum('bqhd,bsd->bqhs', qib, k_idx).astype(jnp.float32)
        I = jnp.einsum('bqh,bqhs->bqs', wb.astype(jnp.float32), jax.nn.relu(dots))
        causal = key_pos[None, :] <= t[:, None]
        I = jnp.where(causal[None], I, -jnp.inf)
        _, sel = lax.top_k(I, top_k)
        valid = sel <= t[None, :, None]
        ks = jax.vmap(lambda kb, ib: kb[ib])(k, sel)
        vs = jax.vmap(lambda vb, ib: vb[ib])(v, sel)
        ps = jax.vmap(lambda pb_, ib: pb_[ib])(pos, sel)
        bias = rel_bias[t5_bucket(pb[:, :, None] - ps)]
        bias = jnp.moveaxis(bias, -1, 2).astype(jnp.float32)
        logits = jnp.einsum('bqhd,bqkd->bqhk', qb, ks).astype(jnp.float32) * scale + bias
        logits = jnp.where(valid[:, :, None, :], logits, -jnp.inf)
        p = jax.nn.softmax(logits, axis=-1).astype(v.dtype)
        return jnp.einsum('bqhk,bqkd->bqhd', p, vs)

    def to_blocks(a):
        return a.reshape((B, n_blk, Q_BLOCK) + a.shape[2:]).swapaxes(0, 1)

    xs = (to_blocks(q), to_blocks(q_idx), to_blocks(w_idx), to_blocks(pos),
          jnp.arange(n_blk) * Q_BLOCK)
    out = lax.map(block, xs)
    return out.swapaxes(0, 1).reshape(B, S, H, Dh)


def retention(q, k, v):
    B, S, H, D = q.shape
    C = RET_CHUNK
    nC = S // C
    log_g = jnp.log1p(-jnp.exp2(-5.0 - jnp.arange(H, dtype=jnp.float32)))
    i = jnp.arange(C, dtype=jnp.float32)
    diff = i[:, None] - i[None, :]
    dmask = jnp.where(diff >= 0, jnp.exp(log_g[:, None, None] * jnp.maximum(diff, 0.0)), 0.0)
    qc = q.astype(jnp.float32).reshape(B, nC, C, H, D)
    kc = k.astype(jnp.float32).reshape(B, nC, C, H, D)
    vc = v.astype(jnp.float32).reshape(B, nC, C, H, D)
    s_in = jnp.einsum('bnihd,bnjhd->bnhij', qc, kc) * dmask
    intra = jnp.einsum('bnhij,bnjhd->bnihd', s_in, vc)
    zeta = jnp.exp(log_g[None, :] * (C - 1.0 - i)[:, None])
    u = jnp.einsum('bnjhd,bnjhe->bnhde', kc * zeta[None, None, :, :, None], vc)
    g_chunk = jnp.exp(log_g * C)[None, :, None, None]

    def step(R, u_n):
        return g_chunk * R + u_n, R

    _, R_prev = lax.scan(step, jnp.zeros((B, H, D, D), jnp.float32), u.swapaxes(0, 1))
    R_prev = R_prev.swapaxes(0, 1)
    xi = jnp.exp(log_g[None, :] * (i + 1.0)[:, None])
    cross = jnp.einsum('bnihd,bnhde->bnihe', qc, R_prev) * xi[None, None, :, :, None]
    return (intra + cross).reshape(B, S, H, D)


def setup_inputs(seed: int = 0) -> dict:
    key = jax.random.key(seed)
    ks = jax.random.split(key, 12)
    f32 = jnp.float32
    x = jax.random.normal(ks[0], (BATCH, SEQ, D_MODEL), f32)
    c = jax.random.normal(ks[1], (BATCH, D_MODEL), f32)
    positions = jnp.broadcast_to(jnp.arange(SEQ, dtype=jnp.int32)[None, :], (BATCH, SEQ))
    rel_bias = 0.5 * jax.random.normal(ks[2], (N_BUCKETS, N_HEADS_A), f32)
    norm_gain = 1.0 + 0.05 * jax.random.normal(ks[3], (DEPTH, D_MODEL), f32)
    w_mod = 0.5 * D_MODEL ** -0.5 * jax.random.normal(ks[4], (DEPTH, D_MODEL, 3 * D_MODEL), f32)
    b_mod = 0.02 * jax.random.normal(ks[5], (DEPTH, 3 * D_MODEL), f32)
    w_in = D_MODEL ** -0.5 * jax.random.normal(ks[6], (DEPTH, D_MODEL, N_IN), f32)
    q_norm_gain = 1.0 + 0.05 * jax.random.normal(ks[7], (DEPTH, HEAD_DIM), f32)
    k_norm_gain = 1.0 + 0.05 * jax.random.normal(ks[8], (DEPTH, HEAD_DIM), f32)
    ret_norm_gain = 1.0 + 0.05 * jax.random.normal(ks[9], (DEPTH, D_B), f32)
    w_out = D_MIX ** -0.5 * jax.random.normal(ks[10], (DEPTH, D_MIX, D_MODEL), f32)
    return {"x": x, "c": c, "positions": positions, "rel_bias": rel_bias,
            "norm_gain": norm_gain, "w_mod": w_mod, "b_mod": b_mod, "w_in": w_in,
            "q_norm_gain": q_norm_gain, "k_norm_gain": k_norm_gain,
            "ret_norm_gain": ret_norm_gain, "w_out": w_out}


def reference(x, c, positions, rel_bias, norm_gain, w_mod, b_mod, w_in,
              q_norm_gain, k_norm_gain, ret_norm_gain, w_out):
    B, S, _ = x.shape
    split_points = [int(p) for p in np.cumsum(IN_SIZES)[:-1]]
    c_act = jax.nn.silu(c)
    for l in range(DEPTH):
        mod = c_act @ w_mod[l] + b_mod[l]
        shift, scale, gate = jnp.split(mod, 3, axis=-1)
        h = rmsnorm(x, norm_gain[l]) * (1.0 + scale[:, None, :]) + shift[:, None, :]
        proj = h @ w_in[l]
        q_a, k_a, v_a, g_a, q_i, k_i, w_i, q_b, k_b, v_b, g_b = jnp.split(proj, split_points, axis=-1)
        q_a = rmsnorm(q_a.reshape(B, S, N_HEADS_A, HEAD_DIM), q_norm_gain[l])
        k_a = rmsnorm(k_a, k_norm_gain[l])
        att = sparse_attention(q_a, k_a, v_a, q_i.reshape(B, S, IDX_HEADS, IDX_DIM), k_i, w_i,
                               positions, rel_bias)
        y_a = att.reshape(B, S, D_A) * jax.nn.silu(g_a)
        q_b = rotary(q_b.reshape(B, S, N_HEADS_B, HEAD_DIM), positions)
        k_b = rotary(k_b.reshape(B, S, N_HEADS_B, HEAD_DIM), positions) * (HEAD_DIM ** -0.5)
        ret = retention(q_b, k_b, v_b.reshape(B, S, N_HEADS_B, HEAD_DIM))
        y_b = head_groupnorm(ret, ret_norm_gain[l]).astype(x.dtype) * jax.nn.silu(g_b)
        y = jnp.concatenate([y_a, y_b], axis=-1) @ w_out[l]
        x = x + gate[:, None, :] * y
    return x
```

```python
import functools
import math

import numpy as np
import jax
import jax.numpy as jnp
from jax import lax
from jax.experimental import pallas as pl
from jax.experimental.pallas import tpu as pltpu

F32 = jnp.float32
BF16 = jnp.bfloat16
I32 = jnp.int32

HEAD_DIM = 128
N_HEADS_A = 8
N_HEADS_B = 8
D_A = N_HEADS_A * HEAD_DIM
D_B = N_HEADS_B * HEAD_DIM
IDX_HEADS = 16
IDX_DIM = 64
TOPK_MAX = 256
RET_CHUNK = 128
N_BUCKETS = 32
MAX_EXACT = 16
MAX_DISTANCE = 128
ROPE_BASE = 10000.0
EPS = 1e-6
IN_SIZES = (D_A, HEAD_DIM, HEAD_DIM, D_A, IDX_HEADS * IDX_DIM, IDX_DIM, IDX_HEADS, D_B, D_B, D_B, D_B)

LANES = 128
VMEM_LIMIT = 56 * 1024 * 1024

SEG_QA, SEG_GA, SEG_QI, SEG_QB, SEG_KB, SEG_VB, SEG_GB = (n * 1024 for n in range(7))
SEG_KA = 7 * 1024
SEG_VA = SEG_KA + LANES
SEG_KIW = SEG_VA + LANES
N_PACK = 7 * 1024 + 512
PROJ_TN = 512

INT_MIN = np.int32(-2 ** 31)
NEG_BIG = -1e30
FAR_REL = 113


def _nt_dot(a, b):
    return lax.dot_general(a, b, (((1,), (1,)), ((), ())), preferred_element_type=F32)


def _mod_kernel(c_ref, w_ref, b_ref, out_ref):
    c = c_ref[...]
    c_act = c * (1.0 / (1.0 + jnp.exp(-c)))
    out_ref[0] = jnp.dot(c_act, w_ref[0], preferred_element_type=F32) + b_ref[0]


def _modulation(c, w_mod, b_mod):
    depth, d, n3 = w_mod.shape
    b = c.shape[0]
    bp = max(8, b)
    c_pad = jnp.zeros((bp, d), F32).at[:b].set(c)
    tn = 768
    out = pl.pallas_call(
        _mod_kernel,
        grid=(depth, n3 // tn),
        in_specs=[pl.BlockSpec((bp, d), lambda l, j: (0, 0)),
                  pl.BlockSpec((1, d, tn), lambda l, j: (l, 0, j)),
                  pl.BlockSpec((1, 1, tn), lambda l, j: (l, 0, j))],
        out_specs=pl.BlockSpec((1, bp, tn), lambda l, j: (l, 0, j)),
        out_shape=jax.ShapeDtypeStruct((depth, bp, n3), F32),
        compiler_params=pltpu.CompilerParams(dimension_semantics=("arbitrary", "arbitrary"),
                                             vmem_limit_bytes=VMEM_LIMIT),
        name="adaln_mod",
    )(c_pad, w_mod, b_mod.reshape(depth, 1, n3))
    return out[:, :b]


def _rope_kernel(pos_ref, inv_ref, sign_ref, out_ref):
    ang = pos_ref[...].astype(F32) * inv_ref[...]
    out_ref[:, :LANES] = jnp.cos(ang)
    out_ref[:, LANES:] = jnp.sin(ang) * sign_ref[...]


def _rope_tables(positions):
    m = positions.size
    half = HEAD_DIM // 2
    inv = ROPE_BASE ** (-jnp.arange(half, dtype=F32) / half)
    inv2 = jnp.concatenate([inv, inv]).reshape(1, LANES)
    sign = jnp.concatenate([-jnp.ones((half,), F32), jnp.ones((half,), F32)]).reshape(1, LANES)
    tm = min(2048, m)
    return pl.pallas_call(
        _rope_kernel,
        grid=(m // tm,),
        in_specs=[pl.BlockSpec((tm, 1), lambda i: (i, 0)),
                  pl.BlockSpec((1, LANES), lambda i: (0, 0)),
                  pl.BlockSpec((1, LANES), lambda i: (0, 0))],
        out_specs=pl.BlockSpec((tm, 2 * LANES), lambda i: (i, 0)),
        out_shape=jax.ShapeDtypeStruct((m, 2 * LANES), F32),
        compiler_params=pltpu.CompilerParams(dimension_semantics=("arbitrary",)),
        name="rope_tables",
    )(positions.reshape(m, 1), inv2, sign)


def _group_rmsnorm(seg, gain):
    ms = jnp.mean(seg * seg, axis=-1, keepdims=True)
    return seg * lax.rsqrt(ms + EPS) * gain


def _proj_kernel(x_ref, g_ref, scale_ref, shift_ref, w_ref, cs_ref, qg_ref, kg_ref,
                 out_ref, kw_ref, h_ref, acc_ref):
    j = pl.program_id(1)

    @pl.when(j == 0)
    def _():
        x = x_ref[...]
        ms = jnp.mean(x * x, axis=-1, keepdims=True)
        y = x * lax.rsqrt(ms + EPS) * g_ref[...]
        h_ref[...] = (y * (1.0 + scale_ref[0]) + shift_ref[0]).astype(BF16)

    acc_ref[...] = jnp.dot(h_ref[...], w_ref[...], preferred_element_type=F32)
    groups = PROJ_TN // LANES
    q_tiles = D_A // PROJ_TN
    rot_lo = SEG_QB // PROJ_TN
    rot_mid = SEG_KB // PROJ_TN
    rot_hi = SEG_VB // PROJ_TN
    last = SEG_KA // PROJ_TN

    @pl.when(j < q_tiles)
    def _():
        gain = qg_ref[...] * (HEAD_DIM ** -0.5)
        for a in range(groups):
            sl = slice(a * LANES, (a + 1) * LANES)
            out_ref[:, sl] = _group_rmsnorm(acc_ref[:, sl], gain).astype(BF16)

    def rotary(scale):
        cos2 = cs_ref[:, :LANES]
        sin2 = cs_ref[:, LANES:]
        for a in range(groups):
            sl = slice(a * LANES, (a + 1) * LANES)
            seg = acc_ref[:, sl]
            rot = seg * cos2 + pltpu.roll(seg, HEAD_DIM // 2, axis=1) * sin2
            out_ref[:, sl] = (rot * scale).astype(BF16)

    @pl.when((j >= rot_lo) & (j < rot_mid))
    def _():
        rotary(1.0)

    @pl.when((j >= rot_mid) & (j < rot_hi))
    def _():
        rotary(HEAD_DIM ** -0.5)

    @pl.when(j == last)
    def _():
        out_ref[:, :LANES] = _group_rmsnorm(acc_ref[:, :LANES], kg_ref[...]).astype(BF16)
        out_ref[:, LANES:] = acc_ref[:, LANES:].astype(BF16)
        kw_ref[...] = acc_ref[:, 2 * LANES:3 * LANES]

    @pl.when(((j >= q_tiles) & (j < rot_lo)) | ((j >= rot_hi) & (j < last)))
    def _():
        out_ref[...] = acc_ref[...].astype(BF16)


def _pack_w_in(w):
    splits = [int(p) for p in np.cumsum(IN_SIZES)[:-1]]
    q_a, k_a, v_a, g_a, q_i, k_i, w_i, q_b, k_b, v_b, g_b = jnp.split(w, splits, axis=-1)
    pad = jnp.zeros((w.shape[0], N_PACK - SEG_KIW - IDX_DIM - IDX_HEADS), w.dtype)
    return jnp.concatenate([q_a, g_a, q_i, q_b, k_b, v_b, g_b, k_a, v_a, k_i, w_i, pad], axis=-1).astype(BF16)


def _in_projection(x2, norm_gain, scale, shift, w_pack, cs, q_gain, k_gain, seq):
    m, d = x2.shape
    tm = min(1024, seq)
    per_batch = seq // tm
    bsz = scale.shape[0]
    return pl.pallas_call(
        _proj_kernel,
        grid=(m // tm, N_PACK // PROJ_TN),
        in_specs=[pl.BlockSpec((tm, d), lambda i, j: (i, 0)),
                  pl.BlockSpec((1, d), lambda i, j: (0, 0)),
                  pl.BlockSpec((1, 1, d), lambda i, j: (i // per_batch, 0, 0)),
                  pl.BlockSpec((1, 1, d), lambda i, j: (i // per_batch, 0, 0)),
                  pl.BlockSpec((d, PROJ_TN), lambda i, j: (0, j)),
                  pl.BlockSpec((tm, 2 * LANES), lambda i, j: (i, 0)),
                  pl.BlockSpec((1, LANES), lambda i, j: (0, 0)),
                  pl.BlockSpec((1, LANES), lambda i, j: (0, 0))],
        out_specs=[pl.BlockSpec((tm, PROJ_TN), lambda i, j: (i, j)),
                   pl.BlockSpec((tm, LANES), lambda i, j: (i, 0))],
        out_shape=[jax.ShapeDtypeStruct((m, N_PACK), BF16),
                   jax.ShapeDtypeStruct((m, LANES), F32)],
        scratch_shapes=[pltpu.VMEM((tm, d), BF16), pltpu.VMEM((tm, PROJ_TN), F32)],
        compiler_params=pltpu.CompilerParams(dimension_semantics=("arbitrary", "arbitrary"),
                                             vmem_limit_bytes=VMEM_LIMIT),
        name="norm_in_proj",
    )(x2, norm_gain.reshape(1, d), scale.reshape(bsz, 1, d), shift.reshape(bsz, 1, d), w_pack, cs,
      q_gain.reshape(1, LANES), k_gain.reshape(1, LANES))


def _attn_kernel(pmin_ref, pmax_ref,
                 q_ref, g_ref, qi_ref, kwq_ref, k_ref, v_ref, kiw_ref, posr_ref, posc_ref, tbl_ref,
                 out_ref,
                 keys_ref, vt_ref, klo_ref, khi_ref, acc_ref, m_ref, l_ref, wt_ref,
                 *, tq, top_k, seq):
    b = pl.program_id(0)
    i = pl.program_id(1)
    tk = tq
    n_kt = i + 1

    @pl.when(i == 0)
    def _():
        def prep(c, carry):
            rows = pl.ds(pl.multiple_of(c * tk, tk), tk)
            vt_ref[:, rows] = v_ref[0, rows, :].astype(F32).T.astype(BF16)
            kf = kiw_ref[0, rows, :].astype(F32)
            lane = lax.broadcasted_iota(I32, kf.shape, 1)
            klo_ref[rows, :] = jnp.where(lane < IDX_DIM, kf, 0.0).astype(BF16)
            khi_ref[rows, :] = jnp.where(lane >= IDX_DIM, pltpu.roll(kf, IDX_DIM, axis=1), 0.0).astype(BF16)
            return carry
        lax.fori_loop(0, seq // tk, prep, 0)

    wt_ref[...] = kwq_ref[0].T[IDX_DIM:IDX_DIM + IDX_HEADS, :]

    t_idx = i * tq + lax.broadcasted_iota(I32, (tk, tq), 1)
    s_loc = lax.broadcasted_iota(I32, (tk, tq), 0)

    def score_tile(j, carry):
        rows = pl.ds(pl.multiple_of(j * tk, tk), tk)
        kl = klo_ref[rows, :]
        kh = khi_ref[rows, :]
        acc = jnp.zeros((tk, tq), F32)
        for p in range(IDX_HEADS // 2):
            slot = qi_ref[0, :, p * LANES:(p + 1) * LANES]
            acc = acc + wt_ref[2 * p:2 * p + 1, :] * jnp.maximum(_nt_dot(kl, slot), 0.0)
            acc = acc + wt_ref[2 * p + 1:2 * p + 2, :] * jnp.maximum(_nt_dot(kh, slot), 0.0)
        bits = pltpu.bitcast(acc, I32)
        key = bits ^ ((bits >> 31) & np.int32(0x7FFFFFFF))
        keys_ref[rows, :] = jnp.where(j * tk + s_loc <= t_idx, key, INT_MIN)
        return carry
    lax.fori_loop(0, n_kt, score_tile, 0)

    def count_ge(cand):
        def body(j, c):
            rows = pl.ds(pl.multiple_of(j * tk, tk), tk)
            hit = (keys_ref[rows, :] >= cand).astype(I32)
            return c + jnp.sum(hit.reshape(tk // 8, 8, tq), axis=0)
        c8 = lax.fori_loop(0, n_kt, body, jnp.zeros((8, tq), I32))
        return jnp.sum(c8, axis=0, keepdims=True)

    zero = jnp.zeros((1, tq), I32)
    base = jnp.where(count_ge(zero) >= top_k, zero, INT_MIN)

    def bit_step(n, base):
        cand = base | (np.int32(1) << (30 - n))
        return jnp.where(count_ge(cand) >= top_k, cand, base)
    thr = lax.fori_loop(0, 31, bit_step, base)
    thr = jnp.maximum(thr, INT_MIN + 1)

    m_ref[...] = jnp.full(m_ref.shape, NEG_BIG, F32)
    l_ref[...] = jnp.zeros(l_ref.shape, F32)
    acc_ref[...] = jnp.zeros(acc_ref.shape, F32)

    def attend(j, with_bias):
        rows = pl.ds(pl.multiple_of(j * tk, tk), tk)
        kt = k_ref[0, rows, :]
        sel = keys_ref[rows, :] >= thr
        if with_bias:
            rel = jnp.clip(posr_ref[0] - posc_ref[0, rows, :], 0, LANES - 1)
        for h in range(N_HEADS_A):
            s = _nt_dot(kt, q_ref[0, :, h * HEAD_DIM:(h + 1) * HEAD_DIM])
            if with_bias:
                tb = jnp.broadcast_to(tbl_ref[h:h + 1, :], (tk, LANES))
                bias = [jnp.take_along_axis(tb, rel[:, c * LANES:(c + 1) * LANES], axis=1)
                        for c in range(tq // LANES)]
                s = s + jnp.concatenate(bias, axis=1)
            s = jnp.where(sel, s, NEG_BIG)
            m_old = m_ref[h]
            m_new = jnp.maximum(m_old, jnp.max(s, axis=0, keepdims=True))
            alpha = jnp.exp(m_old - m_new)
            p = jnp.exp(s - m_new)
            l_ref[h] = alpha * l_ref[h] + jnp.sum(p, axis=0, keepdims=True)
            acc_ref[h] = alpha * acc_ref[h] + jnp.dot(vt_ref[:, rows], p.astype(BF16),
                                                      preferred_element_type=F32)
            m_ref[h] = m_new

    def attend_tile(j, carry):
        near = pmin_ref[b, i] - pmax_ref[b, j] < FAR_REL

        @pl.when(near)
        def _():
            attend(j, True)

        @pl.when(jnp.logical_not(near))
        def _():
            attend(j, False)
        return carry
    lax.fori_loop(0, n_kt, attend_tile, 0)

    for h in range(N_HEADS_A):
        sl = slice(h * HEAD_DIM, (h + 1) * HEAD_DIM)
        o = (acc_ref[h] / l_ref[h]).T
        g = g_ref[0, :, sl].astype(F32)
        out_ref[0, :, sl] = (o * (g / (1.0 + jnp.exp(-g)))).astype(BF16)


def _bias_table(rel_bias):
    r = jnp.arange(LANES, dtype=I32)
    relf = jnp.maximum(r, 1).astype(F32)
    large = MAX_EXACT + (jnp.log(relf / MAX_EXACT) / math.log(MAX_DISTANCE / MAX_EXACT)
                         * (N_BUCKETS - MAX_EXACT)).astype(I32)
    bucket = jnp.where(r < MAX_EXACT, r, jnp.minimum(large, N_BUCKETS - 1))
    return (rel_bias[bucket] - rel_bias[N_BUCKETS - 1][None, :]).T.astype(F32)


def _sparse_attention(proj3, kw3, positions, tbl, pmin, pmax, tq):
    bsz, seq, _ = proj3.shape
    top_k = min(TOPK_MAX, seq // 4)
    n_q = seq // tq
    wide = lambda seg: pl.BlockSpec((1, tq, 1024), lambda b, i, *_: (b, i, seg // 1024))
    full = lambda seg: pl.BlockSpec((1, seq, LANES), lambda b, i, *_: (b, 0, seg // LANES))
    grid_spec = pltpu.PrefetchScalarGridSpec(
        num_scalar_prefetch=2,
        grid=(bsz, n_q),
        in_specs=[wide(SEG_QA), wide(SEG_GA), wide(SEG_QI),
                  pl.BlockSpec((1, tq, LANES), lambda b, i, *_: (b, i, 0)),
                  full(SEG_KA), full(SEG_VA), full(SEG_KIW),
                  pl.BlockSpec((1, 1, tq), lambda b, i, *_: (b, 0, i)),
                  pl.BlockSpec((1, seq, 1), lambda b, i, *_: (b, 0, 0)),
                  pl.BlockSpec((N_HEADS_A, LANES), lambda b, i, *_: (0, 0))],
        out_specs=pl.BlockSpec((1, tq, D_A), lambda b, i, *_: (b, i, 0)),
        scratch_shapes=[pltpu.VMEM((seq, tq), I32),
                        pltpu.VMEM((HEAD_DIM, seq), BF16),
                        pltpu.VMEM((seq, LANES), BF16),
                        pltpu.VMEM((seq, LANES), BF16),
                        pltpu.VMEM((N_HEADS_A, HEAD_DIM, tq), F32),
                        pltpu.VMEM((N_HEADS_A, 1, tq), F32),
                        pltpu.VMEM((N_HEADS_A, 1, tq), F32),
                        pltpu.VMEM((IDX_HEADS, tq), F32)])
    return pl.pallas_call(
        functools.partial(_attn_kernel, tq=tq, top_k=top_k, seq=seq),
        grid_spec=grid_spec,
        out_shape=jax.ShapeDtypeStruct((bsz, seq, D_A), BF16),
        compiler_params=pltpu.CompilerParams(dimension_semantics=("arbitrary", "arbitrary"),
                                             vmem_limit_bytes=VMEM_LIMIT),
        name="sparse_attention",
    )(pmin, pmax, proj3, proj3, proj3, kw3, proj3, proj3, proj3,
      positions.reshape(bsz, 1, seq), positions.reshape(bsz, seq, 1), tbl)


def _ret_kernel(gch_ref, q_ref, k_ref, v_ref, g_ref, dmask_ref, zeta_ref, xi_ref, gain_ref,
                out_ref, state_ref, *, tc):
    c = pl.program_id(1)

    @pl.when(c == 0)
    def _():
        state_ref[...] = jnp.zeros(state_ref.shape, F32)

    for h in range(N_HEADS_B):
        sl = slice(h * HEAD_DIM, (h + 1) * HEAD_DIM)
        state = state_ref[h]
        for cc in range(tc // RET_CHUNK):
            rows = slice(cc * RET_CHUNK, (cc + 1) * RET_CHUNK)
            q = q_ref[0, rows, sl]
            k = k_ref[0, rows, sl]
            v = v_ref[0, rows, sl]
            s_in = _nt_dot(q, k) * dmask_ref[h]
            intra = jnp.dot(s_in.astype(BF16), v, preferred_element_type=F32)
            cross = jnp.dot(q, state.astype(BF16), preferred_element_type=F32) * xi_ref[h]
            o = intra + cross
            kz_t = (k.astype(F32) * zeta_ref[h]).T.astype(BF16)
            state = gch_ref[h] * state + jnp.dot(kz_t, v, preferred_element_type=F32)
            mu = jnp.mean(o, axis=-1, keepdims=True)
            var = jnp.mean(jnp.square(o - mu), axis=-1, keepdims=True)
            y = (o - mu) * lax.rsqrt(var + EPS) * gain_ref[:, sl]
            g = g_ref[0, rows, sl].astype(F32)
            out_ref[0, rows, sl] = (y * (g / (1.0 + jnp.exp(-g)))).astype(BF16)
        state_ref[h] = state


def _retention_tables():
    c = RET_CHUNK
    log_g = jnp.log1p(-jnp.exp2(-5.0 - jnp.arange(N_HEADS_B, dtype=F32)))
    i = jnp.arange(c, dtype=F32)
    diff = i[:, None] - i[None, :]
    dmask = jnp.where(diff >= 0, jnp.exp(log_g[:, None, None] * jnp.maximum(diff, 0.0)), 0.0)
    zeta = jnp.exp(log_g[:, None] * (c - 1.0 - i)[None, :])
    xi = jnp.exp(log_g[:, None] * (i + 1.0)[None, :])
    g_chunk = jnp.exp(log_g * c)
    widen = lambda a: jnp.broadcast_to(a[:, :, None], (N_HEADS_B, c, HEAD_DIM))
    return g_chunk, dmask, widen(zeta), widen(xi)


def _retention(proj3, ret_gain, tables):
    bsz, seq, _ = proj3.shape
    tc = min(512, seq)
    g_chunk, dmask, zeta, xi = tables
    wide = lambda seg: pl.BlockSpec((1, tc, 1024), lambda b, c: (b, c, seg // 1024))
    table = pl.BlockSpec((N_HEADS_B, RET_CHUNK, HEAD_DIM), lambda b, c: (0, 0, 0))
    return pl.pallas_call(
        functools.partial(_ret_kernel, tc=tc),
        grid=(bsz, seq // tc),
        in_specs=[pl.BlockSpec(memory_space=pltpu.SMEM),
                  wide(SEG_QB), wide(SEG_KB), wide(SEG_VB), wide(SEG_GB),
                  table, table, table,
                  pl.BlockSpec((1, D_B), lambda b, c: (0, 0))],
        out_specs=pl.BlockSpec((1, tc, D_B), lambda b, c: (b, c, 0)),
        out_shape=jax.ShapeDtypeStruct((bsz, seq, D_B), BF16),
        scratch_shapes=[pltpu.VMEM((N_HEADS_B, HEAD_DIM, HEAD_DIM), F32)],
        compiler_params=pltpu.CompilerParams(dimension_semantics=("arbitrary", "arbitrary"),
                                             vmem_limit_bytes=VMEM_LIMIT),
        name="retention",
    )(g_chunk, proj3, proj3, proj3, proj3, dmask, zeta, xi, ret_gain.reshape(1, D_B))


def _out_kernel(ya_ref, yb_ref, wa_ref, wb_ref, x_ref, gate_ref, out_ref):
    y = jnp.dot(ya_ref[...], wa_ref[...], preferred_element_type=F32)
    y = y + jnp.dot(yb_ref[...], wb_ref[...], preferred_element_type=F32)
    out_ref[...] = x_ref[...] + gate_ref[0] * y


def _out_projection(ya2, yb2, w_out, x2, gate, seq):
    m, d = x2.shape
    tm = min(1024, seq)
    tn = 1024
    per_batch = seq // tm
    bsz = gate.shape[0]
    return pl.pallas_call(
        _out_kernel,
        grid=(m // tm, d // tn),
        in_specs=[pl.BlockSpec((tm, D_A), lambda i, j: (i, 0)),
                  pl.BlockSpec((tm, D_B), lambda i, j: (i, 0)),
                  pl.BlockSpec((D_A, tn), lambda i, j: (0, j)),
                  pl.BlockSpec((D_B, tn), lambda i, j: (1, j)),
                  pl.BlockSpec((tm, tn), lambda i, j: (i, j)),
                  pl.BlockSpec((1, 1, tn), lambda i, j: (i // per_batch, 0, j))],
        out_specs=pl.BlockSpec((tm, tn), lambda i, j: (i, j)),
        out_shape=jax.ShapeDtypeStruct((m, d), F32),
        compiler_params=pltpu.CompilerParams(dimension_semantics=("arbitrary", "arbitrary"),
                                             vmem_limit_bytes=VMEM_LIMIT),
        name="out_proj_residual",
    )(ya2, yb2, w_out, w_out, x2, gate.reshape(bsz, 1, d))


def kernel(x, c, positions, rel_bias, norm_gain, w_mod, b_mod, w_in, q_norm_gain, k_norm_gain,
           ret_norm_gain, w_out):
    bsz, seq, d = x.shape
    depth = w_in.shape[0]
    m = bsz * seq
    tq = min(256, seq)
    assert d == D_A + D_B and seq % tq == 0 and seq % RET_CHUNK == 0

    mod = _modulation(c, w_mod, b_mod)
    cs = _rope_tables(positions)
    tbl = _bias_table(rel_bias)
    pos_tiles = positions.reshape(bsz, seq // tq, tq)
    pmin = jnp.min(pos_tiles, axis=-1).astype(I32)
    pmax = jnp.max(pos_tiles, axis=-1).astype(I32)
    tables = _retention_tables()

    x2 = x.reshape(m, d)
    for l in range(depth):
        shift, scale, gate = mod[l, :, :d], mod[l, :, d:2 * d], mod[l, :, 2 * d:]
        proj, kw = _in_projection(x2, norm_gain[l], scale, shift, _pack_w_in(w_in[l]), cs,
                                  q_norm_gain[l], k_norm_gain[l], seq)
        proj3 = proj.reshape(bsz, seq, N_PACK)
        y_a = _sparse_attention(proj3, kw.reshape(bsz, seq, LANES), positions, tbl, pmin, pmax, tq)
        y_b = _retention(proj3, ret_norm_gain[l], tables)
        x2 = _out_projection(y_a.reshape(m, D_A), y_b.reshape(m, D_B), w_out[l].astype(BF16), x2, gate, seq)
    return x2.reshape(bsz, seq, d)
```

```python
import functools
import math

import numpy as np
import jax
import jax.numpy as jnp
from jax import lax
from jax.experimental import pallas as pl
from jax.experimental.pallas import tpu as pltpu

F32 = jnp.float32
BF16 = jnp.bfloat16
I32 = jnp.int32

HEAD_DIM = 128
N_HEADS_A = 8
N_HEADS_B = 8
D_A = N_HEADS_A * HEAD_DIM
D_B = N_HEADS_B * HEAD_DIM
IDX_HEADS = 16
IDX_DIM = 64
TOPK_MAX = 256
RET_CHUNK = 128
N_BUCKETS = 32
MAX_EXACT = 16
MAX_DISTANCE = 128
ROPE_BASE = 10000.0
EPS = 1e-6
IN_SIZES = (D_A, HEAD_DIM, HEAD_DIM, D_A, IDX_HEADS * IDX_DIM, IDX_DIM, IDX_HEADS, D_B, D_B, D_B, D_B)

LANES = 128
VMEM_LIMIT = 56 * 1024 * 1024

SEG_QA, SEG_GA, SEG_QI, SEG_QB, SEG_KB, SEG_VB, SEG_GB = (n * 1024 for n in range(7))
SEG_KA = 7 * 1024
SEG_VA = SEG_KA + LANES
SEG_KIW = SEG_VA + LANES
N_PACK = 7 * 1024 + 512
PROJ_TN = 512

INT_MIN = np.int32(-2 ** 31)
INT_MAX = np.int32(2 ** 31 - 1)
LOG2E = math.log2(math.e)
NEG_BIG = -1e30
FAR_REL = 113


def _nt_dot(a, b):
    return lax.dot_general(a, b, (((1,), (1,)), ((), ())), preferred_element_type=F32)


def _mod_kernel(c_ref, w_ref, b_ref, out_ref):
    c = c_ref[...]
    c_act = c * (1.0 / (1.0 + jnp.exp(-c)))
    out_ref[0] = jnp.dot(c_act, w_ref[0], preferred_element_type=F32) + b_ref[0]


def _modulation(c, w_mod, b_mod):
    depth, d, n3 = w_mod.shape
    b = c.shape[0]
    bp = max(8, b)
    c_pad = jnp.zeros((bp, d), F32).at[:b].set(c)
    tn = 768
    out = pl.pallas_call(
        _mod_kernel,
        grid=(depth, n3 // tn),
        in_specs=[pl.BlockSpec((bp, d), lambda l, j: (0, 0)),
                  pl.BlockSpec((1, d, tn), lambda l, j: (l, 0, j)),
                  pl.BlockSpec((1, 1, tn), lambda l, j: (l, 0, j))],
        out_specs=pl.BlockSpec((1, bp, tn), lambda l, j: (l, 0, j)),
        out_shape=jax.ShapeDtypeStruct((depth, bp, n3), F32),
        compiler_params=pltpu.CompilerParams(dimension_semantics=("arbitrary", "arbitrary"),
                                             vmem_limit_bytes=VMEM_LIMIT),
        name="adaln_mod",
    )(c_pad, w_mod, b_mod.reshape(depth, 1, n3))
    return out[:, :b]


def _rope_kernel(pos_ref, inv_ref, sign_ref, out_ref):
    ang = pos_ref[...].astype(F32) * inv_ref[...]
    out_ref[:, :LANES] = jnp.cos(ang)
    out_ref[:, LANES:] = jnp.sin(ang) * sign_ref[...]


def _rope_tables(positions):
    m = positions.size
    half = HEAD_DIM // 2
    inv = ROPE_BASE ** (-jnp.arange(half, dtype=F32) / half)
    inv2 = jnp.concatenate([inv, inv]).reshape(1, LANES)
    sign = jnp.concatenate([-jnp.ones((half,), F32), jnp.ones((half,), F32)]).reshape(1, LANES)
    tm = min(2048, m)
    return pl.pallas_call(
        _rope_kernel,
        grid=(m // tm,),
        in_specs=[pl.BlockSpec((tm, 1), lambda i: (i, 0)),
                  pl.BlockSpec((1, LANES), lambda i: (0, 0)),
                  pl.BlockSpec((1, LANES), lambda i: (0, 0))],
        out_specs=pl.BlockSpec((tm, 2 * LANES), lambda i: (i, 0)),
        out_shape=jax.ShapeDtypeStruct((m, 2 * LANES), F32),
        compiler_params=pltpu.CompilerParams(dimension_semantics=("arbitrary",)),
        name="rope_tables",
    )(positions.reshape(m, 1), inv2, sign)


def _group_rmsnorm(seg, gain):
    ms = jnp.mean(seg * seg, axis=-1, keepdims=True)
    return seg * lax.rsqrt(ms + EPS) * gain


def _proj_kernel(x_ref, g_ref, scale_ref, shift_ref, w_ref, cs_ref, qg_ref, kg_ref,
                 out_ref, kw_ref, h_ref, acc_ref):
    j = pl.program_id(1)

    @pl.when(j == 0)
    def _():
        x = x_ref[...]
        ms = jnp.mean(x * x, axis=-1, keepdims=True)
        y = x * lax.rsqrt(ms + EPS) * g_ref[...]
        h_ref[...] = (y * (1.0 + scale_ref[0]) + shift_ref[0]).astype(BF16)

    acc_ref[...] = jnp.dot(h_ref[...], w_ref[...], preferred_element_type=F32)
    groups = PROJ_TN // LANES
    q_tiles = D_A // PROJ_TN
    rot_lo = SEG_QB // PROJ_TN
    rot_mid = SEG_KB // PROJ_TN
    rot_hi = SEG_VB // PROJ_TN
    last = SEG_KA // PROJ_TN

    @pl.when(j < q_tiles)
    def _():
        gain = qg_ref[...] * (HEAD_DIM ** -0.5 * LOG2E)
        for a in range(groups):
            sl = slice(a * LANES, (a + 1) * LANES)
            out_ref[:, sl] = _group_rmsnorm(acc_ref[:, sl], gain).astype(BF16)

    def rotary(scale):
        cos2 = cs_ref[:, :LANES]
        sin2 = cs_ref[:, LANES:]
        for a in range(groups):
            sl = slice(a * LANES, (a + 1) * LANES)
            seg = acc_ref[:, sl]
            rot = seg * cos2 + pltpu.roll(seg, HEAD_DIM // 2, axis=1) * sin2
            out_ref[:, sl] = (rot * scale).astype(BF16)

    @pl.when((j >= rot_lo) & (j < rot_mid))
    def _():
        rotary(1.0)

    @pl.when((j >= rot_mid) & (j < rot_hi))
    def _():
        rotary(HEAD_DIM ** -0.5)

    @pl.when(j == last)
    def _():
        out_ref[:, :LANES] = _group_rmsnorm(acc_ref[:, :LANES], kg_ref[...]).astype(BF16)
        out_ref[:, LANES:] = acc_ref[:, LANES:].astype(BF16)
        kw_ref[...] = acc_ref[:, 2 * LANES:3 * LANES]

    @pl.when(((j >= q_tiles) & (j < rot_lo)) | ((j >= rot_hi) & (j < last)))
    def _():
        out_ref[...] = acc_ref[...].astype(BF16)


def _pack_w_in(w):
    splits = [int(p) for p in np.cumsum(IN_SIZES)[:-1]]
    q_a, k_a, v_a, g_a, q_i, k_i, w_i, q_b, k_b, v_b, g_b = jnp.split(w, splits, axis=-1)
    pad = jnp.zeros((w.shape[0], N_PACK - SEG_KIW - IDX_DIM - IDX_HEADS), w.dtype)
    return jnp.concatenate([q_a, g_a, q_i, q_b, k_b, v_b, g_b, k_a, v_a, k_i, w_i, pad], axis=-1).astype(BF16)


def _in_projection(x2, norm_gain, scale, shift, w_pack, cs, q_gain, k_gain, seq):
    m, d = x2.shape
    tm = min(1024, seq)
    per_batch = seq // tm
    bsz = scale.shape[0]
    return pl.pallas_call(
        _proj_kernel,
        grid=(m // tm, N_PACK // PROJ_TN),
        in_specs=[pl.BlockSpec((tm, d), lambda i, j: (i, 0)),
                  pl.BlockSpec((1, d), lambda i, j: (0, 0)),
                  pl.BlockSpec((1, 1, d), lambda i, j: (i // per_batch, 0, 0)),
                  pl.BlockSpec((1, 1, d), lambda i, j: (i // per_batch, 0, 0)),
                  pl.BlockSpec((d, PROJ_TN), lambda i, j: (0, j)),
                  pl.BlockSpec((tm, 2 * LANES), lambda i, j: (i, 0)),
                  pl.BlockSpec((1, LANES), lambda i, j: (0, 0)),
                  pl.BlockSpec((1, LANES), lambda i, j: (0, 0))],
        out_specs=[pl.BlockSpec((tm, PROJ_TN), lambda i, j: (i, j)),
                   pl.BlockSpec((tm, LANES), lambda i, j: (i, 0))],
        out_shape=[jax.ShapeDtypeStruct((m, N_PACK), BF16),
                   jax.ShapeDtypeStruct((m, LANES), F32)],
        scratch_shapes=[pltpu.VMEM((tm, d), BF16), pltpu.VMEM((tm, PROJ_TN), F32)],
        compiler_params=pltpu.CompilerParams(dimension_semantics=("arbitrary", "arbitrary"),
                                             vmem_limit_bytes=VMEM_LIMIT),
        name="norm_in_proj",
    )(x2, norm_gain.reshape(1, d), scale.reshape(bsz, 1, d), shift.reshape(bsz, 1, d), w_pack, cs,
      q_gain.reshape(1, LANES), k_gain.reshape(1, LANES))


def _attn_kernel(pmin_ref, pmax_ref,
                 q_ref, g_ref, qi_ref, kwq_ref, k_ref, v_ref, kiw_ref, posr_ref, posc_ref, tbl_ref,
                 out_ref,
                 keys_ref, vt_ref, klo_ref, khi_ref, acc_ref, m_ref, l_ref, wt_ref,
                 pen_ref, s_ref, p_ref, alpha_ref,
                 *, tq, top_k, seq):
    b = pl.program_id(0)
    i = pl.program_id(1)
    tk = tq
    n_kt = i + 1

    @pl.when(i == 0)
    def _():
        def prep(c, carry):
            rows = pl.ds(pl.multiple_of(c * tk, tk), tk)
            vt_ref[:, rows] = v_ref[0, rows, :].astype(F32).T.astype(BF16)
            kf = kiw_ref[0, rows, :].astype(F32)
            lane = lax.broadcasted_iota(I32, kf.shape, 1)
            klo_ref[rows, :] = jnp.where(lane < IDX_DIM, kf, 0.0).astype(BF16)
            khi_ref[rows, :] = jnp.where(lane >= IDX_DIM, pltpu.roll(kf, IDX_DIM, axis=1), 0.0).astype(BF16)
            return carry
        lax.fori_loop(0, seq // tk, prep, 0)

    wt_ref[...] = kwq_ref[0].T[IDX_DIM:IDX_DIM + IDX_HEADS, :]

    t_idx = i * tq + lax.broadcasted_iota(I32, (tk, tq), 1)
    s_loc = lax.broadcasted_iota(I32, (tk, tq), 0)

    def score_tile(j, carry):
        rows = pl.ds(pl.multiple_of(j * tk, tk), tk)
        kl = klo_ref[rows, :]
        kh = khi_ref[rows, :]
        acc = jnp.zeros((tk, tq), F32)
        for p in range(IDX_HEADS // 2):
            slot = qi_ref[0, :, p * LANES:(p + 1) * LANES]
            acc = acc + wt_ref[2 * p:2 * p + 1, :] * jnp.maximum(_nt_dot(kl, slot), 0.0)
            acc = acc + wt_ref[2 * p + 1:2 * p + 2, :] * jnp.maximum(_nt_dot(kh, slot), 0.0)
        bits = pltpu.bitcast(acc, I32)
        key = bits ^ ((bits >> 31) & np.int32(0x7FFFFFFF))
        keys_ref[rows, :] = jnp.where(j * tk + s_loc <= t_idx, key, INT_MIN)
        return carry
    lax.fori_loop(0, n_kt, score_tile, 0)

    def tile_rows(j):
        return pl.ds(pl.multiple_of(j * tk, tk), tk)

    def fold8(x, op):
        return op(x.reshape(tk // 8, 8, tq), axis=0)

    def count_where(pred):
        def body(j, c):
            return c + fold8(pred(j, keys_ref[tile_rows(j), :]).astype(I32), jnp.sum)
        c8 = lax.fori_loop(0, n_kt, body, jnp.zeros((8, tq), I32))
        return jnp.sum(c8, axis=0, keepdims=True)

    def minmax_body(j, c):
        kt = keys_ref[tile_rows(j), :]
        lo8 = fold8(jnp.where(kt == INT_MIN, INT_MAX, kt), jnp.min)
        return jnp.minimum(c[0], lo8), jnp.maximum(c[1], fold8(kt, jnp.max))
    mn8, mx8 = lax.fori_loop(0, n_kt, minmax_body,
                             (jnp.full((8, tq), INT_MAX, I32), jnp.full((8, tq), INT_MIN, I32)))
    key_min = jnp.min(mn8, axis=0, keepdims=True)
    key_max = jnp.max(mx8, axis=0, keepdims=True)

    n_causal = i * tq + lax.broadcasted_iota(I32, (1, tq), 1) + 1
    short = n_causal <= top_k
    lo0 = jnp.where(short, INT_MIN + 1, key_min)
    hi0 = jnp.where(short, INT_MIN + 2, jnp.where(key_max == INT_MAX, key_max, key_max + 1))
    cnt0 = jnp.where(short, top_k, n_causal)

    def active(lo, hi, cnt):
        return (cnt != top_k) & (hi > lo + 1)

    def search_cond(c):
        it, lo, hi, cnt = c
        return (it < 34) & (jnp.max(active(lo, hi, cnt).astype(I32)) > 0)

    def search_body(c):
        it, lo, hi, cnt = c
        piv = jnp.maximum((lo & hi) + ((lo ^ hi) >> 1), lo + 1)
        c_piv = count_where(lambda j, kt: kt >= piv)
        act = active(lo, hi, cnt)
        up = act & (c_piv >= top_k)
        down = act & (c_piv < top_k)
        return (it + 1, jnp.where(up, piv, lo), jnp.where(down, piv, hi), jnp.where(up, c_piv, cnt))
    _, thr, _, cnt_thr = lax.while_loop(search_cond, search_body, (jnp.int32(0), lo0, hi0, cnt0))

    tied = cnt_thr > top_k

    @pl.when(jnp.max(tied.astype(I32)) > 0)
    def _():
        need = top_k - count_where(lambda j, kt: kt > thr)
        s_of = lambda j: j * tk + s_loc

        def cut_step(n, c):
            c_lo, c_hi = c
            mid = (c_lo + c_hi) >> 1
            enough = count_where(lambda j, kt: (kt == thr) & (s_of(j) < mid)) >= need
            return jnp.where(enough, c_lo, mid), jnp.where(enough, mid, c_hi)
        _, cut = lax.fori_loop(0, max(1, (seq - 1).bit_length()), cut_step,
                               (jnp.zeros((1, tq), I32), jnp.full((1, tq), seq, I32)))

        def demote(j, carry):
            kt = keys_ref[tile_rows(j), :]
            drop = tied & (kt == thr) & (s_of(j) >= cut)
            keys_ref[tile_rows(j), :] = jnp.where(drop, INT_MIN, kt)
            return carry
        lax.fori_loop(0, n_kt, demote, 0)

    m_ref[...] = jnp.full(m_ref.shape, NEG_BIG, F32)
    l_ref[...] = jnp.zeros(l_ref.shape, F32)
    acc_ref[...] = jnp.zeros(acc_ref.shape, F32)

    def attend(j, with_bias):
        rows = pl.ds(pl.multiple_of(j * tk, tk), tk)
        kt = k_ref[0, rows, :]
        pen_ref[...] = jnp.where(keys_ref[rows, :] >= thr, 0.0, NEG_BIG)
        if with_bias:
            rel = jnp.clip(posr_ref[0] - posc_ref[0, rows, :], 0, LANES - 1)
        for h in range(N_HEADS_A):
            s = _nt_dot(kt, q_ref[0, :, h * HEAD_DIM:(h + 1) * HEAD_DIM]) + pen_ref[...]
            if with_bias:
                tb = jnp.broadcast_to(tbl_ref[h:h + 1, :], (tk, LANES))
                bias = [jnp.take_along_axis(tb, rel[:, c * LANES:(c + 1) * LANES], axis=1)
                        for c in range(tq // LANES)]
                s = s + jnp.concatenate(bias, axis=1)
            s_ref[h] = s
            m_old = m_ref[h]
            m_new = jnp.maximum(m_old, jnp.max(s, axis=0, keepdims=True))
            alpha_ref[h] = jnp.exp2(m_old - m_new)
            m_ref[h] = m_new
        for h in range(N_HEADS_A):
            p = jnp.exp2(s_ref[h] - m_ref[h])
            p_ref[h] = p.astype(BF16)
            l_ref[h] = alpha_ref[h] * l_ref[h] + jnp.sum(p, axis=0, keepdims=True)
        for h in range(N_HEADS_A):
            acc_ref[h] = alpha_ref[h] * acc_ref[h] + jnp.dot(vt_ref[:, rows], p_ref[h],
                                                             preferred_element_type=F32)

    def attend_tile(j, carry):
        near = pmin_ref[b, i] - pmax_ref[b, j] < FAR_REL

        @pl.when(near)
        def _():
            attend(j, True)

        @pl.when(jnp.logical_not(near))
        def _():
            attend(j, False)
        return carry
    lax.fori_loop(0, n_kt, attend_tile, 0)

    for h in range(N_HEADS_A):
        sl = slice(h * HEAD_DIM, (h + 1) * HEAD_DIM)
        o = (acc_ref[h] / l_ref[h]).T
        g = g_ref[0, :, sl].astype(F32)
        out_ref[0, :, sl] = (o * (g / (1.0 + jnp.exp(-g)))).astype(BF16)


def _bias_table(rel_bias):
    r = jnp.arange(LANES, dtype=I32)
    relf = jnp.maximum(r, 1).astype(F32)
    large = MAX_EXACT + (jnp.log(relf / MAX_EXACT) / math.log(MAX_DISTANCE / MAX_EXACT)
                         * (N_BUCKETS - MAX_EXACT)).astype(I32)
    bucket = jnp.where(r < MAX_EXACT, r, jnp.minimum(large, N_BUCKETS - 1))
    return ((rel_bias[bucket] - rel_bias[N_BUCKETS - 1][None, :]) * LOG2E).T.astype(F32)


def _sparse_attention(proj3, kw3, positions, tbl, pmin, pmax, tq):
    bsz, seq, _ = proj3.shape
    top_k = min(TOPK_MAX, seq // 4)
    n_q = seq // tq
    wide = lambda seg: pl.BlockSpec((1, tq, 1024), lambda b, i, *_: (b, i, seg // 1024))
    full = lambda seg: pl.BlockSpec((1, seq, LANES), lambda b, i, *_: (b, 0, seg // LANES))
    grid_spec = pltpu.PrefetchScalarGridSpec(
        num_scalar_prefetch=2,
        grid=(bsz, n_q),
        in_specs=[wide(SEG_QA), wide(SEG_GA), wide(SEG_QI),
                  pl.BlockSpec((1, tq, LANES), lambda b, i, *_: (b, i, 0)),
                  full(SEG_KA), full(SEG_VA), full(SEG_KIW),
                  pl.BlockSpec((1, 1, tq), lambda b, i, *_: (b, 0, i)),
                  pl.BlockSpec((1, seq, 1), lambda b, i, *_: (b, 0, 0)),
                  pl.BlockSpec((N_HEADS_A, LANES), lambda b, i, *_: (0, 0))],
        out_specs=pl.BlockSpec((1, tq, D_A), lambda b, i, *_: (b, i, 0)),
        scratch_shapes=[pltpu.VMEM((seq, tq), I32),
                        pltpu.VMEM((HEAD_DIM, seq), BF16),
                        pltpu.VMEM((seq, LANES), BF16),
                        pltpu.VMEM((seq, LANES), BF16),
                        pltpu.VMEM((N_HEADS_A, HEAD_DIM, tq), F32),
                        pltpu.VMEM((N_HEADS_A, 1, tq), F32),
                        pltpu.VMEM((N_HEADS_A, 1, tq), F32),
                        pltpu.VMEM((IDX_HEADS, tq), F32),
                        pltpu.VMEM((tq, tq), F32),
                        pltpu.VMEM((N_HEADS_A, tq, tq), F32),
                        pltpu.VMEM((N_HEADS_A, tq, tq), BF16),
                        pltpu.VMEM((N_HEADS_A, 1, tq), F32)])
    return pl.pallas_call(
        functools.partial(_attn_kernel, tq=tq, top_k=top_k, seq=seq),
        grid_spec=grid_spec,
        out_shape=jax.ShapeDtypeStruct((bsz, seq, D_A), BF16),
        compiler_params=pltpu.CompilerParams(dimension_semantics=("arbitrary", "arbitrary"),
                                             vmem_limit_bytes=VMEM_LIMIT),
        name="sparse_attention",
    )(pmin, pmax, proj3, proj3, proj3, kw3, proj3, proj3, proj3,
      positions.reshape(bsz, 1, seq), positions.reshape(bsz, seq, 1), tbl)


def _ret_kernel(gch_ref, q_ref, k_ref, v_ref, g_ref, dmask_ref, zeta_ref, xi_ref, gain_ref,
                out_ref, state_ref, *, tc):
    c = pl.program_id(1)

    @pl.when(c == 0)
    def _():
        state_ref[...] = jnp.zeros(state_ref.shape, F32)

    for h in range(N_HEADS_B):
        sl = slice(h * HEAD_DIM, (h + 1) * HEAD_DIM)
        state = state_ref[h]
        for cc in range(tc // RET_CHUNK):
            rows = slice(cc * RET_CHUNK, (cc + 1) * RET_CHUNK)
            q = q_ref[0, rows, sl]
            k = k_ref[0, rows, sl]
            v = v_ref[0, rows, sl]
            s_in = _nt_dot(q, k) * dmask_ref[h]
            intra = jnp.dot(s_in.astype(BF16), v, preferred_element_type=F32)
            cross = jnp.dot(q, state.astype(BF16), preferred_element_type=F32) * xi_ref[h]
            o = intra + cross
            kz_t = (k.astype(F32) * zeta_ref[h]).T.astype(BF16)
            state = gch_ref[h] * state + jnp.dot(kz_t, v, preferred_element_type=F32)
            mu = jnp.mean(o, axis=-1, keepdims=True)
            var = jnp.mean(jnp.square(o - mu), axis=-1, keepdims=True)
            y = (o - mu) * lax.rsqrt(var + EPS) * gain_ref[:, sl]
            g = g_ref[0, rows, sl].astype(F32)
            out_ref[0, rows, sl] = (y * (g / (1.0 + jnp.exp(-g)))).astype(BF16)
        state_ref[h] = state


def _retention_tables():
    c = RET_CHUNK
    log_g = jnp.log1p(-jnp.exp2(-5.0 - jnp.arange(N_HEADS_B, dtype=F32)))
    i = jnp.arange(c, dtype=F32)
    diff = i[:, None] - i[None, :]
    dmask = jnp.where(diff >= 0, jnp.exp(log_g[:, None, None] * jnp.maximum(diff, 0.0)), 0.0)
    zeta = jnp.exp(log_g[:, None] * (c - 1.0 - i)[None, :])
    xi = jnp.exp(log_g[:, None] * (i + 1.0)[None, :])
    g_chunk = jnp.exp(log_g * c)
    widen = lambda a: jnp.broadcast_to(a[:, :, None], (N_HEADS_B, c, HEAD_DIM))
    return g_chunk, dmask, widen(zeta), widen(xi)


def _retention(proj3, ret_gain, tables):
    bsz, seq, _ = proj3.shape
    tc = min(512, seq)
    g_chunk, dmask, zeta, xi = tables
    wide = lambda seg: pl.BlockSpec((1, tc, 1024), lambda b, c: (b, c, seg // 1024))
    table = pl.BlockSpec((N_HEADS_B, RET_CHUNK, HEAD_DIM), lambda b, c: (0, 0, 0))
    return pl.pallas_call(
        functools.partial(_ret_kernel, tc=tc),
        grid=(bsz, seq // tc),
        in_specs=[pl.BlockSpec(memory_space=pltpu.SMEM),
                  wide(SEG_QB), wide(SEG_KB), wide(SEG_VB), wide(SEG_GB),
                  table, table, table,
                  pl.BlockSpec((1, D_B), lambda b, c: (0, 0))],
        out_specs=pl.BlockSpec((1, tc, D_B), lambda b, c: (b, c, 0)),
        out_shape=jax.ShapeDtypeStruct((bsz, seq, D_B), BF16),
        scratch_shapes=[pltpu.VMEM((N_HEADS_B, HEAD_DIM, HEAD_DIM), F32)],
        compiler_params=pltpu.CompilerParams(dimension_semantics=("arbitrary", "arbitrary"),
                                             vmem_limit_bytes=VMEM_LIMIT),
        name="retention",
    )(g_chunk, proj3, proj3, proj3, proj3, dmask, zeta, xi, ret_gain.reshape(1, D_B))


def _out_kernel(ya_ref, yb_ref, wa_ref, wb_ref, x_ref, gate_ref, out_ref):
    y = jnp.dot(ya_ref[...], wa_ref[...], preferred_element_type=F32)
    y = y + jnp.dot(yb_ref[...], wb_ref[...], preferred_element_type=F32)
    out_ref[...] = x_ref[...] + gate_ref[0] * y


def _out_projection(ya2, yb2, w_out, x2, gate, seq):
    m, d = x2.shape
    tm = min(1024, seq)
    tn = 1024
    per_batch = seq // tm
    bsz = gate.shape[0]
    return pl.pallas_call(
        _out_kernel,
        grid=(m // tm, d // tn),
        in_specs=[pl.BlockSpec((tm, D_A), lambda i, j: (i, 0)),
                  pl.BlockSpec((tm, D_B), lambda i, j: (i, 0)),
                  pl.BlockSpec((D_A, tn), lambda i, j: (0, j)),
                  pl.BlockSpec((D_B, tn), lambda i, j: (1, j)),
                  pl.BlockSpec((tm, tn), lambda i, j: (i, j)),
                  pl.BlockSpec((1, 1, tn), lambda i, j: (i // per_batch, 0, j))],
        out_specs=pl.BlockSpec((tm, tn), lambda i, j: (i, j)),
        out_shape=jax.ShapeDtypeStruct((m, d), F32),
        compiler_params=pltpu.CompilerParams(dimension_semantics=("arbitrary", "arbitrary"),
                                             vmem_limit_bytes=VMEM_LIMIT),
        name="out_proj_residual",
    )(ya2, yb2, w_out, w_out, x2, gate.reshape(bsz, 1, d))


def kernel(x, c, positions, rel_bias, norm_gain, w_mod, b_mod, w_in, q_norm_gain, k_norm_gain,
           ret_norm_gain, w_out):
    bsz, seq, d = x.shape
    depth = w_in.shape[0]
    m = bsz * seq
    tq = min(256, seq)
    assert d == D_A + D_B and seq % tq == 0 and seq % RET_CHUNK == 0

    mod = _modulation(c, w_mod, b_mod)
    cs = _rope_tables(positions)
    tbl = _bias_table(rel_bias)
    pos_tiles = positions.reshape(bsz, seq // tq, tq)
    pmin = jnp.min(pos_tiles, axis=-1).astype(I32)
    pmax = jnp.max(pos_tiles, axis=-1).astype(I32)
    tables = _retention_tables()

    x2 = x.reshape(m, d)
    for l in range(depth):
        shift, scale, gate = mod[l, :, :d], mod[l, :, d:2 * d], mod[l, :, 2 * d:]
        proj, kw = _in_projection(x2, norm_gain[l], scale, shift, _pack_w_in(w_in[l]), cs,
                                  q_norm_gain[l], k_norm_gain[l], seq)
        proj3 = proj.reshape(bsz, seq, N_PACK)
        y_a = _sparse_attention(proj3, kw.reshape(bsz, seq, LANES), positions, tbl, pmin, pmax, tq)
        y_b = _retention(proj3, ret_norm_gain[l], tables)
        x2 = _out_projection(y_a.reshape(m, D_A), y_b.reshape(m, D_B), w_out[l].astype(BF16), x2, gate, seq)
    return x2.reshape(bsz, seq, d)
```

```python
import functools
import math

import numpy as np
import jax
import jax.numpy as jnp
from jax import lax
from jax.experimental import pallas as pl
from jax.experimental.pallas import tpu as pltpu

F32 = jnp.float32
BF16 = jnp.bfloat16
I32 = jnp.int32

HEAD_DIM = 128
N_HEADS_A = 8
N_HEADS_B = 8
D_A = N_HEADS_A * HEAD_DIM
D_B = N_HEADS_B * HEAD_DIM
IDX_HEADS = 16
IDX_DIM = 64
TOPK_MAX = 256
RET_CHUNK = 128
N_BUCKETS = 32
MAX_EXACT = 16
MAX_DISTANCE = 128
ROPE_BASE = 10000.0
EPS = 1e-6
IN_SIZES = (D_A, HEAD_DIM, HEAD_DIM, D_A, IDX_HEADS * IDX_DIM, IDX_DIM, IDX_HEADS, D_B, D_B, D_B, D_B)

LANES = 128
VMEM_LIMIT = 56 * 1024 * 1024

SEG_QA, SEG_GA, SEG_QI, SEG_QB, SEG_KB, SEG_VB, SEG_GB = (n * 1024 for n in range(7))
SEG_KA = 7 * 1024
SEG_VA = SEG_KA + LANES
SEG_KIW = SEG_VA + LANES
N_PACK = 7 * 1024 + 512
PROJ_TN = 512
PROJ_ROWS = 256

INT_MIN = np.int32(-2 ** 31)
INT_MAX = np.int32(2 ** 31 - 1)
LOG2E = math.log2(math.e)
NEG_BIG = -1e30
VT_ROWS = HEAD_DIM + 16
FAR_REL = 113
SEARCH_UNROLL = 3
SEARCH_MAX_STEPS = 33


def _nt_dot(a, b):
    return lax.dot_general(a, b, (((1,), (1,)), ((), ())), preferred_element_type=F32)


def _mod_kernel(c_ref, w_ref, b_ref, out_ref):
    c = c_ref[...]
    c_act = c * (1.0 / (1.0 + jnp.exp(-c)))
    out_ref[0] = jnp.dot(c_act, w_ref[0], preferred_element_type=F32) + b_ref[0]


def _modulation(c, w_mod, b_mod):
    depth, d, n3 = w_mod.shape
    b = c.shape[0]
    bp = max(8, b)
    c_pad = jnp.zeros((bp, d), F32).at[:b].set(c)
    tn = 768
    out = pl.pallas_call(
        _mod_kernel,
        grid=(depth, n3 // tn),
        in_specs=[pl.BlockSpec((bp, d), lambda l, j: (0, 0)),
                  pl.BlockSpec((1, d, tn), lambda l, j: (l, 0, j)),
                  pl.BlockSpec((1, 1, tn), lambda l, j: (l, 0, j))],
        out_specs=pl.BlockSpec((1, bp, tn), lambda l, j: (l, 0, j)),
        out_shape=jax.ShapeDtypeStruct((depth, bp, n3), F32),
        compiler_params=pltpu.CompilerParams(dimension_semantics=("arbitrary", "arbitrary"),
                                             vmem_limit_bytes=VMEM_LIMIT),
        name="adaln_mod",
    )(c_pad, w_mod, b_mod.reshape(depth, 1, n3))
    return out[:, :b]


def _rope_kernel(pos_ref, inv_ref, sign_ref, out_ref):
    ang = pos_ref[...].astype(F32) * inv_ref[...]
    out_ref[:, :LANES] = jnp.cos(ang)
    out_ref[:, LANES:] = jnp.sin(ang) * sign_ref[...]


def _rope_tables(positions):
    m = positions.size
    half = HEAD_DIM // 2
    inv = ROPE_BASE ** (-jnp.arange(half, dtype=F32) / half)
    inv2 = jnp.concatenate([inv, inv]).reshape(1, LANES)
    sign = jnp.concatenate([-jnp.ones((half,), F32), jnp.ones((half,), F32)]).reshape(1, LANES)
    tm = min(2048, m)
    return pl.pallas_call(
        _rope_kernel,
        grid=(m // tm,),
        in_specs=[pl.BlockSpec((tm, 1), lambda i: (i, 0)),
                  pl.BlockSpec((1, LANES), lambda i: (0, 0)),
                  pl.BlockSpec((1, LANES), lambda i: (0, 0))],
        out_specs=pl.BlockSpec((tm, 2 * LANES), lambda i: (i, 0)),
        out_shape=jax.ShapeDtypeStruct((m, 2 * LANES), F32),
        compiler_params=pltpu.CompilerParams(dimension_semantics=("arbitrary",)),
        name="rope_tables",
    )(positions.reshape(m, 1), inv2, sign)


def _group_rmsnorm(seg, gain):
    ms = jnp.mean(seg * seg, axis=-1, keepdims=True)
    return seg * lax.rsqrt(ms + EPS) * gain


def _proj_kernel(x_ref, g_ref, scale_ref, shift_ref, w_ref, cs_ref, out_ref, kw_ref, h_ref):
    j = pl.program_id(1)

    @pl.when(j == 0)
    def _():
        x = x_ref[...]
        ms = jnp.mean(x * x, axis=-1, keepdims=True)
        y = x * lax.rsqrt(ms + EPS) * g_ref[...]
        h_ref[...] = (y * (1.0 + scale_ref[0]) + shift_ref[0]).astype(BF16)

    is_rot = (j >= SEG_QB // PROJ_TN) & (j < SEG_VB // PROJ_TN)
    rot_scale = jnp.where(j >= SEG_KB // PROJ_TN, HEAD_DIM ** -0.5, 1.0)
    kiw_lo = SEG_KIW - SEG_KA
    for r in range(h_ref.shape[0] // PROJ_ROWS):
        rows = slice(r * PROJ_ROWS, (r + 1) * PROJ_ROWS)
        acc = jnp.dot(h_ref[rows, :], w_ref[...], preferred_element_type=F32)
        a_mul = jnp.where(is_rot, cs_ref[rows, :LANES] * rot_scale, 1.0)
        b_mul = jnp.where(is_rot, cs_ref[rows, LANES:] * rot_scale, 0.0)
        for a in range(PROJ_TN // LANES):
            sl = slice(a * LANES, (a + 1) * LANES)
            seg = acc[:, sl]
            out_ref[rows, sl] = (seg * a_mul + pltpu.roll(seg, HEAD_DIM // 2, axis=1) * b_mul).astype(BF16)
        kw_ref[rows, :] = acc[:, kiw_lo:kiw_lo + LANES]


def _pack_w_in(w):
    splits = [int(p) for p in np.cumsum(IN_SIZES)[:-1]]
    q_a, k_a, v_a, g_a, q_i, k_i, w_i, q_b, k_b, v_b, g_b = jnp.split(w, splits, axis=-1)
    pad = jnp.zeros((w.shape[0], N_PACK - SEG_KIW - IDX_DIM - IDX_HEADS), w.dtype)
    return jnp.concatenate([q_a, g_a, q_i, q_b, k_b, v_b, g_b, k_a, v_a, k_i, w_i, pad], axis=-1).astype(BF16)


def _in_projection(x2, norm_gain, scale, shift, w_pack, cs, seq):
    m, d = x2.shape
    tm = min(1024, seq)
    per_batch = seq // tm
    bsz = scale.shape[0]
    return pl.pallas_call(
        _proj_kernel,
        grid=(m // tm, N_PACK // PROJ_TN),
        in_specs=[pl.BlockSpec((tm, d), lambda i, j: (i, 0)),
                  pl.BlockSpec((1, d), lambda i, j: (0, 0)),
                  pl.BlockSpec((1, 1, d), lambda i, j: (i // per_batch, 0, 0)),
                  pl.BlockSpec((1, 1, d), lambda i, j: (i // per_batch, 0, 0)),
                  pl.BlockSpec((d, PROJ_TN), lambda i, j: (0, j)),
                  pl.BlockSpec((tm, 2 * LANES), lambda i, j: (i, 0))],
        out_specs=[pl.BlockSpec((tm, PROJ_TN), lambda i, j: (i, j)),
                   pl.BlockSpec((tm, LANES), lambda i, j: (i, 0))],
        out_shape=[jax.ShapeDtypeStruct((m, N_PACK), BF16),
                   jax.ShapeDtypeStruct((m, LANES), F32)],
        scratch_shapes=[pltpu.VMEM((tm, d), BF16)],
        compiler_params=pltpu.CompilerParams(dimension_semantics=("arbitrary", "arbitrary"),
                                             vmem_limit_bytes=VMEM_LIMIT),
        name="norm_in_proj",
    )(x2, norm_gain.reshape(1, d), scale.reshape(bsz, 1, d), shift.reshape(bsz, 1, d), w_pack, cs)


def _attn_kernel(pmin_ref, pmax_ref,
                 q_ref, g_ref, posr_ref, qi_ref, kwq_ref,
                 k_ref, v_ref, kiw_ref, posc_ref, tbl_ref, qg_ref, kg_ref,
                 out_ref,
                 keys_ref, thr_ref, vt_ref, klo_ref, khi_ref, kn_ref, acc_ref, m_ref, alpha_ref,
                 wt_ref, pen_ref, s_ref, p_ref, qt_ref, qit_ref,
                 *, tq, top_k, seq):
    b = pl.program_id(0)
    i = pl.program_id(1)
    n_q = seq // tq
    tk = tq
    cur = i % 2
    prv = 1 - cur
    i_att = i - 1
    scoring = i < n_q
    attending = i >= 1

    def tile_rows(j):
        return pl.ds(pl.multiple_of(j * tk, tk), tk)

    @pl.when(i == 0)
    def _():
        def prep(c, carry):
            rows = tile_rows(c)
            vt_ref[:HEAD_DIM, rows] = v_ref[0, rows, :].astype(F32).T.astype(BF16)
            row = lax.broadcasted_iota(I32, (VT_ROWS - HEAD_DIM, tk), 0)
            vt_ref[HEAD_DIM:, rows] = jnp.where(row == 0, 1.0, 0.0).astype(BF16)
            kn_ref[rows, :] = _group_rmsnorm(k_ref[0, rows, :].astype(F32), kg_ref[...]).astype(BF16)
            kf = kiw_ref[0, rows, :].astype(F32)
            lane = lax.broadcasted_iota(I32, kf.shape, 1)
            klo_ref[rows, :] = jnp.where(lane < IDX_DIM, kf, 0.0).astype(BF16)
            khi_ref[rows, :] = jnp.where(lane >= IDX_DIM, pltpu.roll(kf, IDX_DIM, axis=1), 0.0).astype(BF16)
            return carry
        lax.fori_loop(0, seq // tk, prep, 0)

    @pl.when(scoring)
    def _():
        wt_ref[...] = kwq_ref[0].T[IDX_DIM:IDX_DIM + IDX_HEADS, :]
        for p in range(IDX_HEADS // 2):
            qit_ref[p] = qi_ref[0, :, p * LANES:(p + 1) * LANES].astype(F32).T.astype(BF16)

    @pl.when(attending)
    def _():
        q_gain = qg_ref[...] * (HEAD_DIM ** -0.5 * LOG2E)
        for h in range(N_HEADS_A):
            qn = _group_rmsnorm(q_ref[0, :, h * HEAD_DIM:(h + 1) * HEAD_DIM].astype(F32), q_gain)
            qt_ref[h] = qn.T.astype(BF16)
        m_ref[...] = jnp.full(m_ref.shape, NEG_BIG, F32)
        acc_ref[...] = jnp.zeros(acc_ref.shape, F32)

    s_loc = lax.broadcasted_iota(I32, (tk, tq), 0)
    t_idx = i * tq + lax.broadcasted_iota(I32, (tk, tq), 1)

    def score_tile(j):
        rows = tile_rows(j)
        kl = klo_ref[rows, :]
        kh = khi_ref[rows, :]
        acc = jnp.zeros((tk, tq), F32)
        for p in range(IDX_HEADS // 2):
            d_lo = jnp.dot(kl, qit_ref[p], preferred_element_type=F32)
            d_hi = jnp.dot(kh, qit_ref[p], preferred_element_type=F32)
            acc = acc + wt_ref[2 * p:2 * p + 1, :] * jnp.maximum(d_lo, 0.0)
            acc = acc + wt_ref[2 * p + 1:2 * p + 2, :] * jnp.maximum(d_hi, 0.0)
        bits = pltpu.bitcast(acc, I32)
        key = bits ^ ((bits >> 31) & np.int32(0x7FFFFFFF))
        keys_ref[cur, rows, :] = jnp.where(j * tk + s_loc <= t_idx, key, INT_MIN)

    def attend_tile(j, with_bias):
        rows = tile_rows(j)
        kt = kn_ref[rows, :]
        pen_ref[...] = jnp.where(keys_ref[prv, rows, :] >= thr_ref[prv], 0.0, NEG_BIG)
        if with_bias:
            rel = jnp.clip(posr_ref[0] - posc_ref[0, rows, :], 0, LANES - 1)
        for h in range(N_HEADS_A):
            s = jnp.dot(kt, qt_ref[h], preferred_element_type=F32) + pen_ref[...]
            if with_bias:
                tb = jnp.broadcast_to(tbl_ref[h:h + 1, :], (tk, LANES))
                bias = [jnp.take_along_axis(tb, rel[:, c * LANES:(c + 1) * LANES], axis=1)
                        for c in range(tq // LANES)]
                s = s + jnp.concatenate(bias, axis=1)
            s_ref[h] = s
            m_old = m_ref[h]
            m_new = jnp.maximum(m_old, jnp.max(s, axis=0, keepdims=True))
            alpha_ref[h] = jnp.exp2(m_old - m_new)
            m_ref[h] = m_new
        for h in range(N_HEADS_A):
            p_ref[h] = jnp.exp2(s_ref[h] - m_ref[h]).astype(BF16)
        for h in range(N_HEADS_A):
            acc_ref[h] = alpha_ref[h] * acc_ref[h] + jnp.dot(vt_ref[:, rows], p_ref[h],
                                                             preferred_element_type=F32)

    def is_near(j):
        return pmin_ref[b, i_att] - pmax_ref[b, j] < FAR_REL

    def fused_tile(j, carry):
        near = is_near(j)

        @pl.when(near)
        def _():
            attend_tile(j, True)
            score_tile(j)

        @pl.when(jnp.logical_not(near))
        def _():
            attend_tile(j, False)
            score_tile(j)
        return carry

    def attend_only_tile(j, carry):
        near = is_near(j)

        @pl.when(near)
        def _():
            attend_tile(j, True)

        @pl.when(jnp.logical_not(near))
        def _():
            attend_tile(j, False)
        return carry

    @pl.when(scoring & attending)
    def _():
        lax.fori_loop(0, i, fused_tile, 0)

    @pl.when(jnp.logical_not(scoring))
    def _():
        lax.fori_loop(0, i, attend_only_tile, 0)

    @pl.when(scoring)
    def _():
        score_tile(i)

    @pl.when(scoring)
    def _():
        n_kt = i + 1

        def fold8(x, op):
            return op(x.reshape(tk // 8, 8, tq), axis=0)

        def count_where(pred):
            def body(j, c):
                return c + fold8(pred(j, keys_ref[cur, tile_rows(j), :]).astype(I32), jnp.sum)
            c8 = lax.fori_loop(0, n_kt, body, jnp.zeros((8, tq), I32))
            return jnp.sum(c8, axis=0, keepdims=True)

        def minmax_body(j, c):
            kt = keys_ref[cur, tile_rows(j), :]
            lo8 = fold8(jnp.where(kt == INT_MIN, INT_MAX, kt), jnp.min)
            return jnp.minimum(c[0], lo8), jnp.maximum(c[1], fold8(kt, jnp.max))
        mn8, mx8 = lax.fori_loop(0, n_kt, minmax_body,
                                 (jnp.full((8, tq), INT_MAX, I32), jnp.full((8, tq), INT_MIN, I32)))
        key_min = jnp.min(mn8, axis=0, keepdims=True)
        key_max = jnp.max(mx8, axis=0, keepdims=True)

        n_causal = i * tq + lax.broadcasted_iota(I32, (1, tq), 1) + 1
        short = n_causal <= top_k
        lo0 = jnp.where(short, INT_MIN + 1, key_min)
        hi0 = jnp.where(short, INT_MIN + 2, jnp.where(key_max == INT_MAX, key_max, key_max + 1))
        cnt0 = jnp.where(short, top_k, n_causal)

        def active(lo, hi, cnt):
            return (cnt != top_k) & (hi > lo + 1)

        def any_lane(mask):
            return jnp.max(jnp.where(mask, 1.0, 0.0)) > 0.0

        def search_cond(c):
            it, lo, hi, cnt = c
            return (it < SEARCH_MAX_STEPS) & any_lane(active(lo, hi, cnt))

        def search_body(c):
            it, lo, hi, cnt = c
            for _ in range(SEARCH_UNROLL):
                piv = jnp.maximum((lo & hi) + ((lo ^ hi) >> 1), lo + 1)
                c_piv = count_where(lambda j, kt: kt >= piv)
                act = active(lo, hi, cnt)
                up = act & (c_piv >= top_k)
                down = act & (c_piv < top_k)
                lo, hi, cnt = jnp.where(up, piv, lo), jnp.where(down, piv, hi), jnp.where(up, c_piv, cnt)
            return it + SEARCH_UNROLL, lo, hi, cnt
        _, thr, _, cnt_thr = lax.while_loop(search_cond, search_body, (jnp.int32(0), lo0, hi0, cnt0))
        thr_ref[cur] = thr

        tied = cnt_thr > top_k

        @pl.when(any_lane(tied))
        def _():
            need = top_k - count_where(lambda j, kt: kt > thr)
            s_of = lambda j: j * tk + s_loc

            def cut_step(n, c):
                c_lo, c_hi = c
                mid = (c_lo + c_hi) >> 1
                enough = count_where(lambda j, kt: (kt == thr) & (s_of(j) < mid)) >= need
                return jnp.where(enough, c_lo, mid), jnp.where(enough, mid, c_hi)
            _, cut = lax.fori_loop(0, max(1, (seq - 1).bit_length()), cut_step,
                                   (jnp.zeros((1, tq), I32), jnp.full((1, tq), seq, I32)))

            def demote(j, carry):
                kt = keys_ref[cur, tile_rows(j), :]
                drop = tied & (kt == thr) & (s_of(j) >= cut)
                keys_ref[cur, tile_rows(j), :] = jnp.where(drop, INT_MIN, kt)
                return carry
            lax.fori_loop(0, n_kt, demote, 0)

    @pl.when(attending)
    def _():
        for h in range(N_HEADS_A):
            sl = slice(h * HEAD_DIM, (h + 1) * HEAD_DIM)
            acc = acc_ref[h]
            o = (acc[:HEAD_DIM] / acc[HEAD_DIM:HEAD_DIM + 1]).T
            g = g_ref[0, :, sl].astype(F32)
            out_ref[0, :, sl] = (o * (g / (1.0 + jnp.exp(-g)))).astype(BF16)


def _bias_table(rel_bias):
    r = jnp.arange(LANES, dtype=I32)
    relf = jnp.maximum(r, 1).astype(F32)
    large = MAX_EXACT + (jnp.log(relf / MAX_EXACT) / math.log(MAX_DISTANCE / MAX_EXACT)
                         * (N_BUCKETS - MAX_EXACT)).astype(I32)
    bucket = jnp.where(r < MAX_EXACT, r, jnp.minimum(large, N_BUCKETS - 1))
    return ((rel_bias[bucket] - rel_bias[N_BUCKETS - 1][None, :]) * LOG2E).T.astype(F32)


def _sparse_attention(proj3, kw3, positions, tbl, pmin, pmax, q_gain, k_gain, tq):
    bsz, seq, _ = proj3.shape
    top_k = min(TOPK_MAX, seq // 4)
    n_q = seq // tq
    att = lambda b, i, *_: (b, jnp.maximum(i - 1, 0))
    sco = lambda b, i, *_: (b, jnp.minimum(i, n_q - 1))
    wide = lambda seg, tile: pl.BlockSpec((1, tq, 1024), lambda *a: (*tile(*a), seg // 1024))
    full = lambda seg: pl.BlockSpec((1, seq, LANES), lambda b, i, *_: (b, 0, seg // LANES))
    fixed = lambda shape: pl.BlockSpec(shape, lambda b, i, *_: (0,) * len(shape))
    grid_spec = pltpu.PrefetchScalarGridSpec(
        num_scalar_prefetch=2,
        grid=(bsz, n_q + 1),
        in_specs=[wide(SEG_QA, att), wide(SEG_GA, att),
                  pl.BlockSpec((1, 1, tq), lambda b, i, *_: (b, 0, jnp.maximum(i - 1, 0))),
                  wide(SEG_QI, sco),
                  pl.BlockSpec((1, tq, LANES), lambda *a: (*sco(*a), 0)),
                  full(SEG_KA), full(SEG_VA), full(SEG_KIW),
                  pl.BlockSpec((1, seq, 1), lambda b, i, *_: (b, 0, 0)),
                  fixed((N_HEADS_A, LANES)), fixed((1, LANES)), fixed((1, LANES))],
        out_specs=pl.BlockSpec((1, tq, D_A), lambda *a: (*att(*a), 0)),
        scratch_shapes=[pltpu.VMEM((2, seq, tq), I32),
                        pltpu.VMEM((2, 1, tq), I32),
                        pltpu.VMEM((VT_ROWS, seq), BF16),
                        pltpu.VMEM((seq, LANES), BF16),
                        pltpu.VMEM((seq, LANES), BF16),
                        pltpu.VMEM((seq, HEAD_DIM), BF16),
                        pltpu.VMEM((N_HEADS_A, VT_ROWS, tq), F32),
                        pltpu.VMEM((N_HEADS_A, 1, tq), F32),
                        pltpu.VMEM((N_HEADS_A, 1, tq), F32),
                        pltpu.VMEM((IDX_HEADS, tq), F32),
                        pltpu.VMEM((tq, tq), F32),
                        pltpu.VMEM((N_HEADS_A, tq, tq), F32),
                        pltpu.VMEM((N_HEADS_A, tq, tq), BF16),
                        pltpu.VMEM((N_HEADS_A, HEAD_DIM, tq), BF16),
                        pltpu.VMEM((IDX_HEADS // 2, LANES, tq), BF16)])
    return pl.pallas_call(
        functools.partial(_attn_kernel, tq=tq, top_k=top_k, seq=seq),
        grid_spec=grid_spec,
        out_shape=jax.ShapeDtypeStruct((bsz, seq, D_A), BF16),
        compiler_params=pltpu.CompilerParams(dimension_semantics=("arbitrary", "arbitrary"),
                                             vmem_limit_bytes=VMEM_LIMIT),
        name="sparse_attention",
    )(pmin, pmax, proj3, proj3, positions.reshape(bsz, 1, seq), proj3, kw3, proj3, proj3, proj3,
      positions.reshape(bsz, seq, 1), tbl, q_gain.reshape(1, LANES), k_gain.reshape(1, LANES))


def _ret_kernel(gch_ref, q_ref, k_ref, v_ref, g_ref, dmask_ref, zeta_ref, xi_ref, gain_ref,
                out_ref, state_ref, *, tc):
    c = pl.program_id(1)

    @pl.when(c == 0)
    def _():
        state_ref[...] = jnp.zeros(state_ref.shape, F32)

    for h in range(N_HEADS_B):
        sl = slice(h * HEAD_DIM, (h + 1) * HEAD_DIM)
        state = state_ref[h]
        for cc in range(tc // RET_CHUNK):
            rows = slice(cc * RET_CHUNK, (cc + 1) * RET_CHUNK)
            q = q_ref[0, rows, sl]
            k = k_ref[0, rows, sl]
            v = v_ref[0, rows, sl]
            s_in = _nt_dot(q, k) * dmask_ref[h]
            intra = jnp.dot(s_in.astype(BF16), v, preferred_element_type=F32)
            cross = jnp.dot(q, state.astype(BF16), preferred_element_type=F32) * xi_ref[h]
            o = intra + cross
            kz_t = (k.astype(F32) * zeta_ref[h]).T.astype(BF16)
            state = gch_ref[h] * state + jnp.dot(kz_t, v, preferred_element_type=F32)
            mu = jnp.mean(o, axis=-1, keepdims=True)
            var = jnp.mean(jnp.square(o - mu), axis=-1, keepdims=True)
            y = (o - mu) * lax.rsqrt(var + EPS) * gain_ref[:, sl]
            g = g_ref[0, rows, sl].astype(F32)
            out_ref[0, rows, sl] = (y * (g / (1.0 + jnp.exp(-g)))).astype(BF16)
        state_ref[h] = state


def _retention_tables():
    c = RET_CHUNK
    log_g = jnp.log1p(-jnp.exp2(-5.0 - jnp.arange(N_HEADS_B, dtype=F32)))
    i = jnp.arange(c, dtype=F32)
    diff = i[:, None] - i[None, :]
    dmask = jnp.where(diff >= 0, jnp.exp(log_g[:, None, None] * jnp.maximum(diff, 0.0)), 0.0)
    zeta = jnp.exp(log_g[:, None] * (c - 1.0 - i)[None, :])
    xi = jnp.exp(log_g[:, None] * (i + 1.0)[None, :])
    g_chunk = jnp.exp(log_g * c)
    widen = lambda a: jnp.broadcast_to(a[:, :, None], (N_HEADS_B, c, HEAD_DIM))
    return g_chunk, dmask, widen(zeta), widen(xi)


def _retention(proj3, ret_gain, tables):
    bsz, seq, _ = proj3.shape
    tc = min(512, seq)
    g_chunk, dmask, zeta, xi = tables
    wide = lambda seg: pl.BlockSpec((1, tc, 1024), lambda b, c: (b, c, seg // 1024))
    table = pl.BlockSpec((N_HEADS_B, RET_CHUNK, HEAD_DIM), lambda b, c: (0, 0, 0))
    return pl.pallas_call(
        functools.partial(_ret_kernel, tc=tc),
        grid=(bsz, seq // tc),
        in_specs=[pl.BlockSpec(memory_space=pltpu.SMEM),
                  wide(SEG_QB), wide(SEG_KB), wide(SEG_VB), wide(SEG_GB),
                  table, table, table,
                  pl.BlockSpec((1, D_B), lambda b, c: (0, 0))],
        out_specs=pl.BlockSpec((1, tc, D_B), lambda b, c: (b, c, 0)),
        out_shape=jax.ShapeDtypeStruct((bsz, seq, D_B), BF16),
        scratch_shapes=[pltpu.VMEM((N_HEADS_B, HEAD_DIM, HEAD_DIM), F32)],
        compiler_params=pltpu.CompilerParams(dimension_semantics=("arbitrary", "arbitrary"),
                                             vmem_limit_bytes=VMEM_LIMIT),
        name="retention",
    )(g_chunk, proj3, proj3, proj3, proj3, dmask, zeta, xi, ret_gain.reshape(1, D_B))


def _out_kernel(ya_ref, yb_ref, wa_ref, wb_ref, x_ref, gate_ref, out_ref):
    y = jnp.dot(ya_ref[...], wa_ref[...], preferred_element_type=F32)
    y = y + jnp.dot(yb_ref[...], wb_ref[...], preferred_element_type=F32)
    out_ref[...] = x_ref[...] + gate_ref[0] * y


def _out_projection(ya2, yb2, w_out, x2, gate, seq):
    m, d = x2.shape
    tm = min(1024, seq)
    tn = 1024
    per_batch = seq // tm
    bsz = gate.shape[0]
    return pl.pallas_call(
        _out_kernel,
        grid=(m // tm, d // tn),
        in_specs=[pl.BlockSpec((tm, D_A), lambda i, j: (i, 0)),
                  pl.BlockSpec((tm, D_B), lambda i, j: (i, 0)),
                  pl.BlockSpec((D_A, tn), lambda i, j: (0, j)),
                  pl.BlockSpec((D_B, tn), lambda i, j: (1, j)),
                  pl.BlockSpec((tm, tn), lambda i, j: (i, j)),
                  pl.BlockSpec((1, 1, tn), lambda i, j: (i // per_batch, 0, j))],
        out_specs=pl.BlockSpec((tm, tn), lambda i, j: (i, j)),
        out_shape=jax.ShapeDtypeStruct((m, d), F32),
        compiler_params=pltpu.CompilerParams(dimension_semantics=("arbitrary", "arbitrary"),
                                             vmem_limit_bytes=VMEM_LIMIT),
        name="out_proj_residual",
    )(ya2, yb2, w_out, w_out, x2, gate.reshape(bsz, 1, d))


def kernel(x, c, positions, rel_bias, norm_gain, w_mod, b_mod, w_in, q_norm_gain, k_norm_gain,
           ret_norm_gain, w_out):
    bsz, seq, d = x.shape
    depth = w_in.shape[0]
    m = bsz * seq
    tq = min(256, seq)
    assert d == D_A + D_B and seq % tq == 0 and seq % RET_CHUNK == 0

    mod = _modulation(c, w_mod, b_mod)
    cs = _rope_tables(positions)
    tbl = _bias_table(rel_bias)
    pos_tiles = positions.reshape(bsz, seq // tq, tq)
    pmin = jnp.min(pos_tiles, axis=-1).astype(I32)
    pmax = jnp.max(pos_tiles, axis=-1).astype(I32)
    tables = _retention_tables()

    x2 = x.reshape(m, d)
    for l in range(depth):
        shift, scale, gate = mod[l, :, :d], mod[l, :, d:2 * d], mod[l, :, 2 * d:]
        proj, kw = _in_projection(x2, norm_gain[l], scale, shift, _pack_w_in(w_in[l]), cs, seq)
        proj3 = proj.reshape(bsz, seq, N_PACK)
        y_a = _sparse_attention(proj3, kw.reshape(bsz, seq, LANES), positions, tbl, pmin, pmax,
                                q_norm_gain[l], k_norm_gain[l], tq)
        y_b = _retention(proj3, ret_norm_gain[l], tables)
        x2 = _out_projection(y_a.reshape(m, D_A), y_b.reshape(m, D_B), w_out[l].astype(BF16), x2, gate, seq)
    return x2.reshape(bsz, seq, d)
```

```python
import functools
import math

import numpy as np
import jax
import jax.numpy as jnp
from jax import lax
from jax.experimental import pallas as pl
from jax.experimental.pallas import tpu as pltpu

F32 = jnp.float32
BF16 = jnp.bfloat16
I32 = jnp.int32

HEAD_DIM = 128
N_HEADS_A = 8
N_HEADS_B = 8
D_A = N_HEADS_A * HEAD_DIM
D_B = N_HEADS_B * HEAD_DIM
IDX_HEADS = 16
IDX_DIM = 64
TOPK_MAX = 256
RET_CHUNK = 128
N_BUCKETS = 32
MAX_EXACT = 16
MAX_DISTANCE = 128
ROPE_BASE = 10000.0
EPS = 1e-6
IN_SIZES = (D_A, HEAD_DIM, HEAD_DIM, D_A, IDX_HEADS * IDX_DIM, IDX_DIM, IDX_HEADS, D_B, D_B, D_B, D_B)

LANES = 128
VMEM_LIMIT = 56 * 1024 * 1024

SEG_QA, SEG_GA, SEG_QI, SEG_QB, SEG_KB, SEG_VB, SEG_GB = (n * 1024 for n in range(7))
SEG_KA = 7 * 1024
SEG_VA = SEG_KA + LANES
SEG_KIW = SEG_VA + LANES
N_PACK = 7 * 1024 + 512
PROJ_TN = 1536
PROJ_SUB = 512
PROJ_ROWS = 256

INT_MIN = np.int32(-2 ** 31)
INT_MAX = np.int32(2 ** 31 - 1)
LOG2E = math.log2(math.e)
NEG_BIG = -1e30
VT_ROWS = HEAD_DIM + 16
FAR_REL = 113
SLOTS_PER_SUM = 1
COUNT_CHAINS = 4
SEARCH_UNROLL = 3
SEARCH_MAX_STEPS = 33


def _nt_dot(a, b):
    return lax.dot_general(a, b, (((1,), (1,)), ((), ())), preferred_element_type=F32)


def _mod_kernel(c_ref, w_ref, b_ref, out_ref):
    c = c_ref[...]
    c_act = c * (1.0 / (1.0 + jnp.exp(-c)))
    out_ref[0] = jnp.dot(c_act, w_ref[0], preferred_element_type=F32) + b_ref[0]


def _modulation(c, w_mod, b_mod):
    depth, d, n3 = w_mod.shape
    b = c.shape[0]
    bp = max(8, b)
    c_pad = jnp.zeros((bp, d), F32).at[:b].set(c)
    tn = 768
    out = pl.pallas_call(
        _mod_kernel,
        grid=(depth, n3 // tn),
        in_specs=[pl.BlockSpec((bp, d), lambda l, j: (0, 0)),
                  pl.BlockSpec((1, d, tn), lambda l, j: (l, 0, j)),
                  pl.BlockSpec((1, 1, tn), lambda l, j: (l, 0, j))],
        out_specs=pl.BlockSpec((1, bp, tn), lambda l, j: (l, 0, j)),
        out_shape=jax.ShapeDtypeStruct((depth, bp, n3), F32),
        compiler_params=pltpu.CompilerParams(dimension_semantics=("arbitrary", "arbitrary"),
                                             vmem_limit_bytes=VMEM_LIMIT),
        name="adaln_mod",
    )(c_pad, w_mod, b_mod.reshape(depth, 1, n3))
    return out[:, :b]


def _rope_kernel(pos_ref, inv_ref, sign_ref, out_ref):
    ang = pos_ref[...].astype(F32) * inv_ref[...]
    out_ref[:, :LANES] = jnp.cos(ang)
    out_ref[:, LANES:] = jnp.sin(ang) * sign_ref[...]


def _rope_tables(positions):
    m = positions.size
    half = HEAD_DIM // 2
    inv = ROPE_BASE ** (-jnp.arange(half, dtype=F32) / half)
    inv2 = jnp.concatenate([inv, inv]).reshape(1, LANES)
    sign = jnp.concatenate([-jnp.ones((half,), F32), jnp.ones((half,), F32)]).reshape(1, LANES)
    tm = min(2048, m)
    return pl.pallas_call(
        _rope_kernel,
        grid=(m // tm,),
        in_specs=[pl.BlockSpec((tm, 1), lambda i: (i, 0)),
                  pl.BlockSpec((1, LANES), lambda i: (0, 0)),
                  pl.BlockSpec((1, LANES), lambda i: (0, 0))],
        out_specs=pl.BlockSpec((tm, 2 * LANES), lambda i: (i, 0)),
        out_shape=jax.ShapeDtypeStruct((m, 2 * LANES), F32),
        compiler_params=pltpu.CompilerParams(dimension_semantics=("arbitrary",)),
        name="rope_tables",
    )(positions.reshape(m, 1), inv2, sign)


def _group_rmsnorm(seg, gain):
    ms = jnp.mean(seg * seg, axis=-1, keepdims=True)
    return seg * lax.rsqrt(ms + EPS) * gain


def _proj_kernel(x_ref, g_ref, scale_ref, shift_ref, w_ref, cs_ref, out_ref, kw_ref, h_ref):
    j = pl.program_id(1)
    subs = PROJ_TN // PROJ_SUB
    kiw_sub, kiw_lo = divmod(SEG_KIW % PROJ_TN, PROJ_SUB)

    def project(rows, h):
        for c in range(subs):
            sub = j * subs + c
            is_rot = (sub >= SEG_QB // PROJ_SUB) & (sub < SEG_VB // PROJ_SUB)
            rot_scale = jnp.where(sub >= SEG_KB // PROJ_SUB, HEAD_DIM ** -0.5, 1.0)
            acc = jnp.dot(h, w_ref[:, c * PROJ_SUB:(c + 1) * PROJ_SUB], preferred_element_type=F32)
            a_mul = jnp.where(is_rot, cs_ref[rows, :LANES] * rot_scale, 1.0)
            b_mul = jnp.where(is_rot, cs_ref[rows, LANES:] * rot_scale, 0.0)
            for a in range(PROJ_SUB // LANES):
                seg = acc[:, a * LANES:(a + 1) * LANES]
                col = c * PROJ_SUB + a * LANES
                out_ref[rows, col:col + LANES] = (
                    seg * a_mul + pltpu.roll(seg, HEAD_DIM // 2, axis=1) * b_mul).astype(BF16)
            if c == kiw_sub:
                kw_ref[rows, :] = acc[:, kiw_lo:kiw_lo + LANES]

    chunks = [slice(r * PROJ_ROWS, (r + 1) * PROJ_ROWS) for r in range(h_ref.shape[0] // PROJ_ROWS)]

    @pl.when(j == 0)
    def _():
        for rows in chunks:
            x = x_ref[rows, :]
            ms = jnp.mean(x * x, axis=-1, keepdims=True)
            y = x * lax.rsqrt(ms + EPS) * g_ref[...]
            h = (y * (1.0 + scale_ref[0]) + shift_ref[0]).astype(BF16)
            h_ref[rows, :] = h
            project(rows, h)

    @pl.when(j != 0)
    def _():
        for rows in chunks:
            project(rows, h_ref[rows, :])


def _pack_w_in(w):
    splits = [int(p) for p in np.cumsum(IN_SIZES)[:-1]]
    q_a, k_a, v_a, g_a, q_i, k_i, w_i, q_b, k_b, v_b, g_b = jnp.split(w, splits, axis=-1)
    pad = jnp.zeros(w.shape[:-1] + (N_PACK - SEG_KIW - IDX_DIM - IDX_HEADS,), w.dtype)
    parts = [q_a, g_a, q_i, q_b, k_b, v_b, g_b, k_a, v_a, k_i, w_i, pad]
    return jnp.concatenate([p.astype(BF16) for p in parts], axis=-1)


def _in_projection(x2, norm_gain, scale, shift, w_pack, cs, seq):
    m, d = x2.shape
    tm = min(1024, seq)
    per_batch = seq // tm
    bsz = scale.shape[0]
    return pl.pallas_call(
        _proj_kernel,
        grid=(m // tm, N_PACK // PROJ_TN),
        in_specs=[pl.BlockSpec((tm, d), lambda i, j: (i, 0)),
                  pl.BlockSpec((1, d), lambda i, j: (0, 0)),
                  pl.BlockSpec((1, 1, d), lambda i, j: (i // per_batch, 0, 0)),
                  pl.BlockSpec((1, 1, d), lambda i, j: (i // per_batch, 0, 0)),
                  pl.BlockSpec((d, PROJ_TN), lambda i, j: (0, j)),
                  pl.BlockSpec((tm, 2 * LANES), lambda i, j: (i, 0))],
        out_specs=[pl.BlockSpec((tm, PROJ_TN), lambda i, j: (i, j)),
                   pl.BlockSpec((tm, LANES), lambda i, j: (i, 0))],
        out_shape=[jax.ShapeDtypeStruct((m, N_PACK), BF16),
                   jax.ShapeDtypeStruct((m, LANES), F32)],
        scratch_shapes=[pltpu.VMEM((tm, d), BF16)],
        compiler_params=pltpu.CompilerParams(dimension_semantics=("arbitrary", "arbitrary"),
                                             vmem_limit_bytes=VMEM_LIMIT),
        name="norm_in_proj",
    )(x2, norm_gain.reshape(1, d), scale.reshape(bsz, 1, d), shift.reshape(bsz, 1, d), w_pack, cs)


def _attn_kernel(pmin_ref, pmax_ref,
                 q_ref, g_ref, posr_ref, qi_ref, kwq_ref,
                 k_ref, v_ref, kiw_ref, posc_ref, tbl_ref, qg_ref, kg_ref,
                 out_ref,
                 keys_ref, thr_ref, vt_ref, klo_ref, khi_ref, kn_ref, acc_ref, m_ref, alpha_ref,
                 wt_ref, pen_ref, sc_ref, s_ref, p_ref, qt_ref, qit_ref,
                 *, tq, top_k, seq):
    b = pl.program_id(0)
    i = pl.program_id(1)
    n_q = seq // tq
    tk = tq
    cur = i % 2
    prv = 1 - cur
    i_att = i - 1
    scoring = i < n_q
    attending = i >= 1

    def tile_rows(j):
        return pl.ds(pl.multiple_of(j * tk, tk), tk)

    @pl.when(i == 0)
    def _():
        def prep(c, carry):
            rows = tile_rows(c)
            vt_ref[:HEAD_DIM, rows] = v_ref[0, rows, :].astype(F32).T.astype(BF16)
            row = lax.broadcasted_iota(I32, (VT_ROWS - HEAD_DIM, tk), 0)
            vt_ref[HEAD_DIM:, rows] = jnp.where(row == 0, 1.0, 0.0).astype(BF16)
            kn_ref[rows, :] = _group_rmsnorm(k_ref[0, rows, :].astype(F32), kg_ref[...]).astype(BF16)
            kf = kiw_ref[0, rows, :].astype(F32)
            lane = lax.broadcasted_iota(I32, kf.shape, 1)
            klo_ref[rows, :] = jnp.where(lane < IDX_DIM, kf, 0.0).astype(BF16)
            khi_ref[rows, :] = jnp.where(lane >= IDX_DIM, pltpu.roll(kf, IDX_DIM, axis=1), 0.0).astype(BF16)
            return carry
        lax.fori_loop(0, seq // tk, prep, 0)

    @pl.when(scoring)
    def _():
        wt_ref[...] = kwq_ref[0].T[IDX_DIM:IDX_DIM + IDX_HEADS, :]
        for p in range(IDX_HEADS // 2):
            qit_ref[p] = qi_ref[0, :, p * LANES:(p + 1) * LANES].astype(F32).T.astype(BF16)

    @pl.when(attending)
    def _():
        q_gain = qg_ref[...] * (HEAD_DIM ** -0.5 * LOG2E)
        for h in range(N_HEADS_A):
            qn = _group_rmsnorm(q_ref[0, :, h * HEAD_DIM:(h + 1) * HEAD_DIM].astype(F32), q_gain)
            qt_ref[h] = qn.T.astype(BF16)
        m_ref[...] = jnp.full(m_ref.shape, NEG_BIG, F32)
        acc_ref[...] = jnp.zeros(acc_ref.shape, F32)


    def score_slots(j, slots):
        rows = tile_rows(j)
        term = None
        for p in slots:
            d_lo = jnp.dot(klo_ref[rows, :], qit_ref[p], preferred_element_type=F32)
            d_hi = jnp.dot(khi_ref[rows, :], qit_ref[p], preferred_element_type=F32)
            t = (wt_ref[2 * p:2 * p + 1, :] * jnp.maximum(d_lo, 0.0)
                 + wt_ref[2 * p + 1:2 * p + 2, :] * jnp.maximum(d_hi, 0.0))
            term = t if term is None else term + t
        sc_ref[...] = term if slots[0] == 0 else sc_ref[...] + term

    def score_finish(j, diagonal):
        bits = pltpu.bitcast(sc_ref[...], I32)
        key = bits ^ ((bits >> 31) & np.int32(0x7FFFFFFF))
        if diagonal:
            s_loc = lax.broadcasted_iota(I32, (tk, tq), 0)
            t_loc = lax.broadcasted_iota(I32, (tk, tq), 1)
            key = jnp.where(s_loc <= t_loc, key, INT_MIN)
        keys_ref[cur, tile_rows(j), :] = key

    def attend_tile(j, with_bias, score=False):
        rows = tile_rows(j)
        kt = kn_ref[rows, :]
        pen_ref[...] = jnp.where(keys_ref[prv, rows, :] >= thr_ref[prv], 0.0, NEG_BIG)
        if with_bias:
            rel = jnp.clip(posr_ref[0] - posc_ref[0, rows, :], 0, LANES - 1)
        for h in range(N_HEADS_A):
            s = jnp.dot(kt, qt_ref[h], preferred_element_type=F32) + pen_ref[...]
            if with_bias:
                tb = jnp.broadcast_to(tbl_ref[h:h + 1, :], (tk, LANES))
                bias = [jnp.take_along_axis(tb, rel[:, c * LANES:(c + 1) * LANES], axis=1)
                        for c in range(tq // LANES)]
                s = s + jnp.concatenate(bias, axis=1)
            s_ref[h] = s
            m_old = m_ref[h]
            m_new = jnp.maximum(m_old, jnp.max(s_ref[h], axis=0, keepdims=True))
            alpha_ref[h] = jnp.exp2(m_old - m_new)
            m_ref[h] = m_new
        for h in range(N_HEADS_A):
            p_ref[h] = jnp.exp2(s_ref[h] - m_ref[h]).astype(BF16)
            if score and h % SLOTS_PER_SUM == SLOTS_PER_SUM - 1:
                score_slots(j, tuple(range(h + 1 - SLOTS_PER_SUM, h + 1)))
        for h in range(N_HEADS_A):
            acc_ref[h] = alpha_ref[h] * acc_ref[h] + jnp.dot(vt_ref[:, rows], p_ref[h],
                                                             preferred_element_type=F32)
        if score:
            score_finish(j, diagonal=False)

    def is_near(j):
        return pmin_ref[b, i_att] - pmax_ref[b, j] < FAR_REL

    def fused_tile(j, carry):
        near = is_near(j)

        @pl.when(near)
        def _():
            attend_tile(j, True, score=True)

        @pl.when(jnp.logical_not(near))
        def _():
            attend_tile(j, False, score=True)
        return carry

    def attend_only_tile(j, carry):
        near = is_near(j)

        @pl.when(near)
        def _():
            attend_tile(j, True)

        @pl.when(jnp.logical_not(near))
        def _():
            attend_tile(j, False)
        return carry

    @pl.when(scoring & attending)
    def _():
        lax.fori_loop(0, i, fused_tile, 0)

    @pl.when(jnp.logical_not(scoring))
    def _():
        lax.fori_loop(0, i, attend_only_tile, 0)

    @pl.when(scoring)
    def _():
        for p in range(0, IDX_HEADS // 2, SLOTS_PER_SUM):
            score_slots(i, tuple(range(p, p + SLOTS_PER_SUM)))
        score_finish(i, diagonal=True)

    @pl.when(scoring)
    def _():
        n_kt = i + 1

        def fold8(x, op):
            return op(x.reshape(tk // 8, 8, tq), axis=0)

        def count_where(pred):
            def body(j, c):
                hit = pred(j, keys_ref[cur, tile_rows(j), :]).astype(I32)
                return c + jnp.sum(hit.reshape(COUNT_CHAINS, tk // (8 * COUNT_CHAINS), 8, tq), axis=1)
            c8 = lax.fori_loop(0, n_kt, body, jnp.zeros((COUNT_CHAINS, 8, tq), I32))
            return jnp.sum(jnp.sum(c8, axis=0), axis=0, keepdims=True)

        def minmax_body(j, c):
            kt = keys_ref[cur, tile_rows(j), :]
            lo8 = fold8(jnp.where(kt == INT_MIN, INT_MAX, kt), jnp.min)
            return jnp.minimum(c[0], lo8), jnp.maximum(c[1], fold8(kt, jnp.max))
        mn8, mx8 = lax.fori_loop(0, n_kt, minmax_body,
                                 (jnp.full((8, tq), INT_MAX, I32), jnp.full((8, tq), INT_MIN, I32)))
        key_min = jnp.min(mn8, axis=0, keepdims=True)
        key_max = jnp.max(mx8, axis=0, keepdims=True)

        n_causal = i * tq + lax.broadcasted_iota(I32, (1, tq), 1) + 1
        short = n_causal <= top_k
        lo0 = jnp.where(short, INT_MIN + 1, key_min)
        hi0 = jnp.where(short, INT_MIN + 2, jnp.where(key_max == INT_MAX, key_max, key_max + 1))
        cnt0 = jnp.where(short, top_k, n_causal)

        def active(lo, hi, cnt):
            return (cnt != top_k) & (hi > lo + 1)

        def any_lane(mask):
            return jnp.max(jnp.where(mask, 1.0, 0.0)) > 0.0

        def search_cond(c):
            it, lo, hi, cnt = c
            return (it < SEARCH_MAX_STEPS) & any_lane(active(lo, hi, cnt))

        def search_body(c):
            it, lo, hi, cnt = c
            for _ in range(SEARCH_UNROLL):
                piv = jnp.maximum((lo & hi) + ((lo ^ hi) >> 1), lo + 1)
                c_piv = count_where(lambda j, kt: kt >= piv)
                act = active(lo, hi, cnt)
                up = act & (c_piv >= top_k)
                down = act & (c_piv < top_k)
                lo, hi, cnt = jnp.where(up, piv, lo), jnp.where(down, piv, hi), jnp.where(up, c_piv, cnt)
            return it + SEARCH_UNROLL, lo, hi, cnt
        _, thr, _, cnt_thr = lax.while_loop(search_cond, search_body, (jnp.int32(0), lo0, hi0, cnt0))
        thr_ref[cur] = thr

        tied = cnt_thr > top_k

        @pl.when(any_lane(tied))
        def _():
            need = top_k - count_where(lambda j, kt: kt > thr)
            s_of = lambda j: j * tk + lax.broadcasted_iota(I32, (tk, tq), 0)

            def cut_step(n, c):
                c_lo, c_hi = c
                mid = (c_lo + c_hi) >> 1
                enough = count_where(lambda j, kt: (kt == thr) & (s_of(j) < mid)) >= need
                return jnp.where(enough, c_lo, mid), jnp.where(enough, mid, c_hi)
            _, cut = lax.fori_loop(0, max(1, (seq - 1).bit_length()), cut_step,
                                   (jnp.zeros((1, tq), I32), jnp.full((1, tq), seq, I32)))

            def demote(j, carry):
                kt = keys_ref[cur, tile_rows(j), :]
                drop = tied & (kt == thr) & (s_of(j) >= cut)
                keys_ref[cur, tile_rows(j), :] = jnp.where(drop, INT_MIN, kt)
                return carry
            lax.fori_loop(0, n_kt, demote, 0)

    @pl.when(attending)
    def _():
        for h in range(N_HEADS_A):
            sl = slice(h * HEAD_DIM, (h + 1) * HEAD_DIM)
            acc = acc_ref[h]
            o = (acc[:HEAD_DIM] / acc[HEAD_DIM:HEAD_DIM + 1]).T
            g = g_ref[0, :, sl].astype(F32)
            out_ref[0, :, sl] = (o * (g / (1.0 + jnp.exp(-g)))).astype(BF16)


def _bias_table(rel_bias):
    r = jnp.arange(LANES, dtype=I32)
    relf = jnp.maximum(r, 1).astype(F32)
    large = MAX_EXACT + (jnp.log(relf / MAX_EXACT) / math.log(MAX_DISTANCE / MAX_EXACT)
                         * (N_BUCKETS - MAX_EXACT)).astype(I32)
    bucket = jnp.where(r < MAX_EXACT, r, jnp.minimum(large, N_BUCKETS - 1))
    return ((rel_bias[bucket] - rel_bias[N_BUCKETS - 1][None, :]) * LOG2E).T.astype(F32)


def _sparse_attention(proj3, kw3, positions, tbl, pmin, pmax, q_gain, k_gain, tq):
    bsz, seq, _ = proj3.shape
    top_k = min(TOPK_MAX, seq // 4)
    n_q = seq // tq
    att = lambda b, i, *_: (b, jnp.maximum(i - 1, 0))
    sco = lambda b, i, *_: (b, jnp.minimum(i, n_q - 1))
    wide = lambda seg, tile: pl.BlockSpec((1, tq, 1024), lambda *a: (*tile(*a), seg // 1024))
    full = lambda seg: pl.BlockSpec((1, seq, LANES), lambda b, i, *_: (b, 0, seg // LANES))
    fixed = lambda shape: pl.BlockSpec(shape, lambda b, i, *_: (0,) * len(shape))
    grid_spec = pltpu.PrefetchScalarGridSpec(
        num_scalar_prefetch=2,
        grid=(bsz, n_q + 1),
        in_specs=[wide(SEG_QA, att), wide(SEG_GA, att),
                  pl.BlockSpec((1, 1, tq), lambda b, i, *_: (b, 0, jnp.maximum(i - 1, 0))),
                  wide(SEG_QI, sco),
                  pl.BlockSpec((1, tq, LANES), lambda *a: (*sco(*a), 0)),
                  full(SEG_KA), full(SEG_VA), full(SEG_KIW),
                  pl.BlockSpec((1, seq, 1), lambda b, i, *_: (b, 0, 0)),
                  fixed((N_HEADS_A, LANES)), fixed((1, LANES)), fixed((1, LANES))],
        out_specs=pl.BlockSpec((1, tq, D_A), lambda *a: (*att(*a), 0)),
        scratch_shapes=[pltpu.VMEM((2, seq, tq), I32),
                        pltpu.VMEM((2, 1, tq), I32),
                        pltpu.VMEM((VT_ROWS, seq), BF16),
                        pltpu.VMEM((seq, LANES), BF16),
                        pltpu.VMEM((seq, LANES), BF16),
                        pltpu.VMEM((seq, HEAD_DIM), BF16),
                        pltpu.VMEM((N_HEADS_A, VT_ROWS, tq), F32),
                        pltpu.VMEM((N_HEADS_A, 1, tq), F32),
                        pltpu.VMEM((N_HEADS_A, 1, tq), F32),
                        pltpu.VMEM((IDX_HEADS, tq), F32),
                        pltpu.VMEM((tq, tq), F32),
                        pltpu.VMEM((tq, tq), F32),
                        pltpu.VMEM((N_HEADS_A, tq, tq), F32),
                        pltpu.VMEM((N_HEADS_A, tq, tq), BF16),
                        pltpu.VMEM((N_HEADS_A, HEAD_DIM, tq), BF16),
                        pltpu.VMEM((IDX_HEADS // 2, LANES, tq), BF16)])
    return pl.pallas_call(
        functools.partial(_attn_kernel, tq=tq, top_k=top_k, seq=seq),
        grid_spec=grid_spec,
        out_shape=jax.ShapeDtypeStruct((bsz, seq, D_A), BF16),
        compiler_params=pltpu.CompilerParams(dimension_semantics=("arbitrary", "arbitrary"),
                                             vmem_limit_bytes=VMEM_LIMIT),
        name="sparse_attention",
    )(pmin, pmax, proj3, proj3, positions.reshape(bsz, 1, seq), proj3, kw3, proj3, proj3, proj3,
      positions.reshape(bsz, seq, 1), tbl, q_gain.reshape(1, LANES), k_gain.reshape(1, LANES))


def _ret_kernel(gch_ref, q_ref, k_ref, v_ref, g_ref, dmask_ref, zeta_ref, xi_ref, gain_ref,
                out_ref, state_ref, intra_ref, u_ref, *, tc):
    c = pl.program_id(1)
    chunks = [slice(cc * RET_CHUNK, (cc + 1) * RET_CHUNK) for cc in range(tc // RET_CHUNK)]
    heads = [slice(h * HEAD_DIM, (h + 1) * HEAD_DIM) for h in range(N_HEADS_B)]

    @pl.when(c == 0)
    def _():
        state_ref[...] = jnp.zeros(state_ref.shape, F32)

    for h, sl in enumerate(heads):
        for cc, rows in enumerate(chunks):
            q = q_ref[0, rows, sl]
            k = k_ref[0, rows, sl]
            v = v_ref[0, rows, sl]
            s_in = _nt_dot(q, k) * dmask_ref[h]
            intra_ref[h, cc] = jnp.dot(s_in.astype(BF16), v, preferred_element_type=F32)
            kz_t = (k.astype(F32) * zeta_ref[h]).T.astype(BF16)
            u_ref[h, cc] = jnp.dot(kz_t, v, preferred_element_type=F32)

    for h, sl in enumerate(heads):
        state = state_ref[h]
        for cc, rows in enumerate(chunks):
            cross = jnp.dot(q_ref[0, rows, sl], state.astype(BF16), preferred_element_type=F32) * xi_ref[h]
            o = intra_ref[h, cc] + cross
            state = gch_ref[h] * state + u_ref[h, cc]
            mu = jnp.mean(o, axis=-1, keepdims=True)
            var = jnp.mean(jnp.square(o - mu), axis=-1, keepdims=True)
            y = (o - mu) * lax.rsqrt(var + EPS) * gain_ref[:, sl]
            g = g_ref[0, rows, sl].astype(F32)
            out_ref[0, rows, sl] = (y * (g / (1.0 + jnp.exp(-g)))).astype(BF16)
        state_ref[h] = state


def _retention_tables():
    c = RET_CHUNK
    log_g = jnp.log1p(-jnp.exp2(-5.0 - jnp.arange(N_HEADS_B, dtype=F32)))
    i = jnp.arange(c, dtype=F32)
    diff = i[:, None] - i[None, :]
    dmask = jnp.where(diff >= 0, jnp.exp(log_g[:, None, None] * jnp.maximum(diff, 0.0)), 0.0)
    zeta = jnp.exp(log_g[:, None] * (c - 1.0 - i)[None, :])
    xi = jnp.exp(log_g[:, None] * (i + 1.0)[None, :])
    g_chunk = jnp.exp(log_g * c)
    widen = lambda a: jnp.broadcast_to(a[:, :, None], (N_HEADS_B, c, HEAD_DIM))
    return g_chunk, dmask, widen(zeta), widen(xi)


def _retention(proj3, ret_gain, tables):
    bsz, seq, _ = proj3.shape
    tc = min(512, seq)
    g_chunk, dmask, zeta, xi = tables
    wide = lambda seg: pl.BlockSpec((1, tc, 1024), lambda b, c: (b, c, seg // 1024))
    table = pl.BlockSpec((N_HEADS_B, RET_CHUNK, HEAD_DIM), lambda b, c: (0, 0, 0))
    return pl.pallas_call(
        functools.partial(_ret_kernel, tc=tc),
        grid=(bsz, seq // tc),
        in_specs=[pl.BlockSpec(memory_space=pltpu.SMEM),
                  wide(SEG_QB), wide(SEG_KB), wide(SEG_VB), wide(SEG_GB),
                  table, table, table,
                  pl.BlockSpec((1, D_B), lambda b, c: (0, 0))],
        out_specs=pl.BlockSpec((1, tc, D_B), lambda b, c: (b, c, 0)),
        out_shape=jax.ShapeDtypeStruct((bsz, seq, D_B), BF16),
        scratch_shapes=[pltpu.VMEM((N_HEADS_B, HEAD_DIM, HEAD_DIM), F32),
                        pltpu.VMEM((N_HEADS_B, tc // RET_CHUNK, RET_CHUNK, HEAD_DIM), F32),
                        pltpu.VMEM((N_HEADS_B, tc // RET_CHUNK, HEAD_DIM, HEAD_DIM), F32)],
        compiler_params=pltpu.CompilerParams(dimension_semantics=("arbitrary", "arbitrary"),
                                             vmem_limit_bytes=VMEM_LIMIT),
        name="retention",
    )(g_chunk, proj3, proj3, proj3, proj3, dmask, zeta, xi, ret_gain.reshape(1, D_B))


def _out_kernel(ya_ref, yb_ref, wa_ref, wb_ref, x_ref, gate_ref, out_ref):
    y = jnp.dot(ya_ref[...], wa_ref[...], preferred_element_type=F32)
    y = y + jnp.dot(yb_ref[...], wb_ref[...], preferred_element_type=F32)
    out_ref[...] = x_ref[...] + gate_ref[0] * y


def _out_projection(ya2, yb2, w_out, x2, gate, seq):
    m, d = x2.shape
    tm = min(1024, seq)
    tn = 1024
    per_batch = seq // tm
    bsz = gate.shape[0]
    return pl.pallas_call(
        _out_kernel,
        grid=(m // tm, d // tn),
        in_specs=[pl.BlockSpec((tm, D_A), lambda i, j: (i, 0)),
                  pl.BlockSpec((tm, D_B), lambda i, j: (i, 0)),
                  pl.BlockSpec((D_A, tn), lambda i, j: (0, j)),
                  pl.BlockSpec((D_B, tn), lambda i, j: (1, j)),
                  pl.BlockSpec((tm, tn), lambda i, j: (i, j)),
                  pl.BlockSpec((1, 1, tn), lambda i, j: (i // per_batch, 0, j))],
        out_specs=pl.BlockSpec((tm, tn), lambda i, j: (i, j)),
        out_shape=jax.ShapeDtypeStruct((m, d), F32),
        compiler_params=pltpu.CompilerParams(dimension_semantics=("arbitrary", "arbitrary"),
                                             vmem_limit_bytes=VMEM_LIMIT),
        name="out_proj_residual",
    )(ya2, yb2, w_out, w_out, x2, gate.reshape(bsz, 1, d))


def kernel(x, c, positions, rel_bias, norm_gain, w_mod, b_mod, w_in, q_norm_gain, k_norm_gain,
           ret_norm_gain, w_out):
    bsz, seq, d = x.shape
    depth = w_in.shape[0]
    m = bsz * seq
    tq = min(256, seq)
    assert d == D_A + D_B and seq % tq == 0 and seq % RET_CHUNK == 0

    mod = _modulation(c, w_mod, b_mod)
    cs = _rope_tables(positions)
    tbl = _bias_table(rel_bias)
    pos_tiles = positions.reshape(bsz, seq // tq, tq)
    pmin = jnp.min(pos_tiles, axis=-1).astype(I32)
    pmax = jnp.max(pos_tiles, axis=-1).astype(I32)
    tables = _retention_tables()
    w_in_packed = _pack_w_in(w_in)
    w_out_bf16 = w_out.astype(BF16)

    x2 = x.reshape(m, d)
    for l in range(depth):
        shift, scale, gate = mod[l, :, :d], mod[l, :, d:2 * d], mod[l, :, 2 * d:]
        proj, kw = _in_projection(x2, norm_gain[l], scale, shift, w_in_packed[l], cs, seq)
        proj3 = proj.reshape(bsz, seq, N_PACK)
        y_a = _sparse_attention(proj3, kw.reshape(bsz, seq, LANES), positions, tbl, pmin, pmax,
                                q_norm_gain[l], k_norm_gain[l], tq)
        y_b = _retention(proj3, ret_norm_gain[l], tables)
        x2 = _out_projection(y_a.reshape(m, D_A), y_b.reshape(m, D_B), w_out_bf16[l], x2, gate, seq)
    return x2.reshape(bsz, seq, d)
```

```python
import functools
import math

import numpy as np
import jax
import jax.numpy as jnp
from jax import lax
from jax.experimental import pallas as pl
from jax.experimental.pallas import tpu as pltpu

F32 = jnp.float32
BF16 = jnp.bfloat16
I32 = jnp.int32

HEAD_DIM = 128
N_HEADS_A = 8
N_HEADS_B = 8
D_A = N_HEADS_A * HEAD_DIM
D_B = N_HEADS_B * HEAD_DIM
IDX_HEADS = 16
IDX_DIM = 64
TOPK_MAX = 256
RET_CHUNK = 128
N_BUCKETS = 32
MAX_EXACT = 16
MAX_DISTANCE = 128
ROPE_BASE = 10000.0
EPS = 1e-6
IN_SIZES = (D_A, HEAD_DIM, HEAD_DIM, D_A, IDX_HEADS * IDX_DIM, IDX_DIM, IDX_HEADS, D_B, D_B, D_B, D_B)

LANES = 128
VMEM_LIMIT = 56 * 1024 * 1024

SEG_QA, SEG_GA, SEG_QI, SEG_QB, SEG_KB, SEG_VB, SEG_GB = (n * 1024 for n in range(7))
SEG_KA = 7 * 1024
SEG_VA = SEG_KA + LANES
SEG_KIW = SEG_VA + LANES
N_PACK = 7 * 1024 + 512
PROJ_TN = 1536
PROJ_SUB = 512
PROJ_ROWS = 256

INT_MIN = np.int32(-2 ** 31)
INT_MAX = np.int32(2 ** 31 - 1)
LOG2E = math.log2(math.e)
NEG_BIG = -1e30
VT_ROWS = HEAD_DIM + 16
FAR_REL = 113
SLOTS_PER_SUM = 1
COUNT_CHAINS = 4
SEARCH_UNROLL = 3
SEARCH_VALUE_STEPS = 12
SEARCH_MAX_STEPS = SEARCH_VALUE_STEPS + 33


def _sortable(x):
    bits = pltpu.bitcast(x, I32)
    return bits ^ ((bits >> 31) & np.int32(0x7FFFFFFF))


def _unsortable(key):
    return pltpu.bitcast(key ^ ((key >> 31) & np.int32(0x7FFFFFFF)), F32)


def _nt_dot(a, b):
    return lax.dot_general(a, b, (((1,), (1,)), ((), ())), preferred_element_type=F32)


def _mod_kernel(c_ref, w_ref, b_ref, out_ref):
    c = c_ref[...]
    c_act = c * (1.0 / (1.0 + jnp.exp(-c)))
    out_ref[0] = jnp.dot(c_act, w_ref[0], preferred_element_type=F32) + b_ref[0]


def _modulation(c, w_mod, b_mod):
    depth, d, n3 = w_mod.shape
    b = c.shape[0]
    bp = max(8, b)
    c_pad = jnp.zeros((bp, d), F32).at[:b].set(c)
    tn = 768
    out = pl.pallas_call(
        _mod_kernel,
        grid=(depth, n3 // tn),
        in_specs=[pl.BlockSpec((bp, d), lambda l, j: (0, 0)),
                  pl.BlockSpec((1, d, tn), lambda l, j: (l, 0, j)),
                  pl.BlockSpec((1, 1, tn), lambda l, j: (l, 0, j))],
        out_specs=pl.BlockSpec((1, bp, tn), lambda l, j: (l, 0, j)),
        out_shape=jax.ShapeDtypeStruct((depth, bp, n3), F32),
        compiler_params=pltpu.CompilerParams(dimension_semantics=("arbitrary", "arbitrary"),
                                             vmem_limit_bytes=VMEM_LIMIT),
        name="adaln_mod",
    )(c_pad, w_mod, b_mod.reshape(depth, 1, n3))
    return out[:, :b]


def _rope_kernel(pos_ref, inv_ref, sign_ref, out_ref):
    ang = pos_ref[...].astype(F32) * inv_ref[...]
    out_ref[:, :LANES] = jnp.cos(ang)
    out_ref[:, LANES:] = jnp.sin(ang) * sign_ref[...]


def _rope_tables(positions):
    m = positions.size
    half = HEAD_DIM // 2
    inv = ROPE_BASE ** (-jnp.arange(half, dtype=F32) / half)
    inv2 = jnp.concatenate([inv, inv]).reshape(1, LANES)
    sign = jnp.concatenate([-jnp.ones((half,), F32), jnp.ones((half,), F32)]).reshape(1, LANES)
    tm = min(2048, m)
    return pl.pallas_call(
        _rope_kernel,
        grid=(m // tm,),
        in_specs=[pl.BlockSpec((tm, 1), lambda i: (i, 0)),
                  pl.BlockSpec((1, LANES), lambda i: (0, 0)),
                  pl.BlockSpec((1, LANES), lambda i: (0, 0))],
        out_specs=pl.BlockSpec((tm, 2 * LANES), lambda i: (i, 0)),
        out_shape=jax.ShapeDtypeStruct((m, 2 * LANES), F32),
        compiler_params=pltpu.CompilerParams(dimension_semantics=("arbitrary",)),
        name="rope_tables",
    )(positions.reshape(m, 1), inv2, sign)


def _group_rmsnorm(seg, gain):
    ms = jnp.mean(seg * seg, axis=-1, keepdims=True)
    return seg * lax.rsqrt(ms + EPS) * gain


def _proj_kernel(x_ref, g_ref, scale_ref, shift_ref, w_ref, cs_ref, out_ref, kw_ref, h_ref):
    j = pl.program_id(1)
    subs = PROJ_TN // PROJ_SUB
    kiw_sub, kiw_lo = divmod(SEG_KIW % PROJ_TN, PROJ_SUB)

    def project(rows, h):
        for c in range(subs):
            sub = j * subs + c
            is_rot = (sub >= SEG_QB // PROJ_SUB) & (sub < SEG_VB // PROJ_SUB)
            rot_scale = jnp.where(sub >= SEG_KB // PROJ_SUB, HEAD_DIM ** -0.5, 1.0)
            acc = jnp.dot(h, w_ref[0, :, c * PROJ_SUB:(c + 1) * PROJ_SUB], preferred_element_type=F32)
            a_mul = jnp.where(is_rot, cs_ref[rows, :LANES] * rot_scale, 1.0)
            b_mul = jnp.where(is_rot, cs_ref[rows, LANES:] * rot_scale, 0.0)
            for a in range(PROJ_SUB // LANES):
                seg = acc[:, a * LANES:(a + 1) * LANES]
                col = c * PROJ_SUB + a * LANES
                out_ref[rows, col:col + LANES] = (
                    seg * a_mul + pltpu.roll(seg, HEAD_DIM // 2, axis=1) * b_mul).astype(BF16)
            if c == kiw_sub:
                kw_ref[rows, :] = acc[:, kiw_lo:kiw_lo + LANES]

    chunks = [slice(r * PROJ_ROWS, (r + 1) * PROJ_ROWS) for r in range(h_ref.shape[0] // PROJ_ROWS)]

    @pl.when(j == 0)
    def _():
        for rows in chunks:
            x = x_ref[rows, :]
            ms = jnp.mean(x * x, axis=-1, keepdims=True)
            y = x * lax.rsqrt(ms + EPS) * g_ref[...]
            h = (y * (1.0 + scale_ref[0]) + shift_ref[0]).astype(BF16)
            h_ref[rows, :] = h
            project(rows, h)

    @pl.when(j != 0)
    def _():
        for rows in chunks:
            project(rows, h_ref[rows, :])


def _pack_w_in(w):
    splits = [int(p) for p in np.cumsum(IN_SIZES)[:-1]]
    q_a, k_a, v_a, g_a, q_i, k_i, w_i, q_b, k_b, v_b, g_b = jnp.split(w, splits, axis=-1)
    pad = jnp.zeros(w.shape[:-1] + (N_PACK - SEG_KIW - IDX_DIM - IDX_HEADS,), w.dtype)
    parts = [q_a, g_a, q_i, q_b, k_b, v_b, g_b, k_a, v_a, k_i, w_i, pad]
    return jnp.concatenate([p.astype(BF16) for p in parts], axis=-1)


def _in_projection(x2, norm_gain, scale, shift, w_pack, layer, cs, seq):
    m, d = x2.shape
    tm = min(1024, seq)
    per_batch = seq // tm
    bsz = scale.shape[0]
    return pl.pallas_call(
        _proj_kernel,
        grid=(m // tm, N_PACK // PROJ_TN),
        in_specs=[pl.BlockSpec((tm, d), lambda i, j: (i, 0)),
                  pl.BlockSpec((1, d), lambda i, j: (0, 0)),
                  pl.BlockSpec((1, 1, d), lambda i, j: (i // per_batch, 0, 0)),
                  pl.BlockSpec((1, 1, d), lambda i, j: (i // per_batch, 0, 0)),
                  pl.BlockSpec((1, d, PROJ_TN), lambda i, j: (layer, 0, j)),
                  pl.BlockSpec((tm, 2 * LANES), lambda i, j: (i, 0))],
        out_specs=[pl.BlockSpec((tm, PROJ_TN), lambda i, j: (i, j)),
                   pl.BlockSpec((tm, LANES), lambda i, j: (i, 0))],
        out_shape=[jax.ShapeDtypeStruct((m, N_PACK), BF16),
                   jax.ShapeDtypeStruct((m, LANES), F32)],
        scratch_shapes=[pltpu.VMEM((tm, d), BF16)],
        compiler_params=pltpu.CompilerParams(dimension_semantics=("arbitrary", "arbitrary"),
                                             vmem_limit_bytes=VMEM_LIMIT),
        name="norm_in_proj",
    )(x2, norm_gain.reshape(1, d), scale.reshape(bsz, 1, d), shift.reshape(bsz, 1, d), w_pack, cs)


def _attn_kernel(pmin_ref, pmax_ref,
                 q_ref, g_ref, posr_ref, qi_ref, kwq_ref,
                 k_ref, v_ref, kiw_ref, posc_ref, tbl_ref, qg_ref, kg_ref,
                 out_ref,
                 keys_ref, thr_ref, vt_ref, klo_ref, khi_ref, kn_ref, acc_ref, m_ref, alpha_ref,
                 wt_ref, pen_ref, sc_ref, s_ref, p_ref, qt_ref, qit_ref,
                 *, tq, top_k, seq):
    b = pl.program_id(0)
    i = pl.program_id(1)
    n_q = seq // tq
    tk = tq
    cur = i % 2
    prv = 1 - cur
    i_att = i - 1
    scoring = i < n_q
    attending = i >= 1

    def tile_rows(j):
        return pl.ds(pl.multiple_of(j * tk, tk), tk)

    @pl.when(i == 0)
    def _():
        def prep(c, carry):
            rows = tile_rows(c)
            vt_ref[:HEAD_DIM, rows] = v_ref[0, rows, :].astype(F32).T.astype(BF16)
            row = lax.broadcasted_iota(I32, (VT_ROWS - HEAD_DIM, tk), 0)
            vt_ref[HEAD_DIM:, rows] = jnp.where(row == 0, 1.0, 0.0).astype(BF16)
            kn_ref[rows, :] = _group_rmsnorm(k_ref[0, rows, :].astype(F32), kg_ref[...]).astype(BF16)
            kf = kiw_ref[0, rows, :].astype(F32)
            lane = lax.broadcasted_iota(I32, kf.shape, 1)
            klo_ref[rows, :] = jnp.where(lane < IDX_DIM, kf, 0.0).astype(BF16)
            khi_ref[rows, :] = jnp.where(lane >= IDX_DIM, pltpu.roll(kf, IDX_DIM, axis=1), 0.0).astype(BF16)
            return carry
        lax.fori_loop(0, seq // tk, prep, 0)

    @pl.when(scoring)
    def _():
        wt_ref[...] = kwq_ref[0].T[IDX_DIM:IDX_DIM + IDX_HEADS, :]
        for p in range(IDX_HEADS // 2):
            qit_ref[p] = qi_ref[0, :, p * LANES:(p + 1) * LANES].astype(F32).T.astype(BF16)

    @pl.when(attending)
    def _():
        q_gain = qg_ref[...] * (HEAD_DIM ** -0.5 * LOG2E)
        for h in range(N_HEADS_A):
            qn = _group_rmsnorm(q_ref[0, :, h * HEAD_DIM:(h + 1) * HEAD_DIM].astype(F32), q_gain)
            qt_ref[h] = qn.T.astype(BF16)
        m_ref[...] = jnp.full(m_ref.shape, NEG_BIG, F32)
        acc_ref[...] = jnp.zeros(acc_ref.shape, F32)


    def score_slots(j, slots):
        rows = tile_rows(j)
        term = None
        for p in slots:
            d_lo = jnp.dot(klo_ref[rows, :], qit_ref[p], preferred_element_type=F32)
            d_hi = jnp.dot(khi_ref[rows, :], qit_ref[p], preferred_element_type=F32)
            t = (wt_ref[2 * p:2 * p + 1, :] * jnp.maximum(d_lo, 0.0)
                 + wt_ref[2 * p + 1:2 * p + 2, :] * jnp.maximum(d_hi, 0.0))
            term = t if term is None else term + t
        sc_ref[...] = term if slots[0] == 0 else sc_ref[...] + term

    def score_finish(j, diagonal):
        key = _sortable(sc_ref[...])
        if diagonal:
            s_loc = lax.broadcasted_iota(I32, (tk, tq), 0)
            t_loc = lax.broadcasted_iota(I32, (tk, tq), 1)
            key = jnp.where(s_loc <= t_loc, key, INT_MIN)
        keys_ref[cur, tile_rows(j), :] = key

    def attend_tile(j, with_bias, score=False):
        rows = tile_rows(j)
        kt = kn_ref[rows, :]
        pen_ref[...] = jnp.where(keys_ref[prv, rows, :] >= thr_ref[prv], 0.0, NEG_BIG)
        if with_bias:
            rel = jnp.clip(posr_ref[0] - posc_ref[0, rows, :], 0, LANES - 1)
        for h in range(N_HEADS_A):
            s = jnp.dot(kt, qt_ref[h], preferred_element_type=F32) + pen_ref[...]
            if with_bias:
                tb = jnp.broadcast_to(tbl_ref[h:h + 1, :], (tk, LANES))
                bias = [jnp.take_along_axis(tb, rel[:, c * LANES:(c + 1) * LANES], axis=1)
                        for c in range(tq // LANES)]
                s = s + jnp.concatenate(bias, axis=1)
            s_ref[h] = s
            m_old = m_ref[h]
            m_new = jnp.maximum(m_old, jnp.max(s_ref[h], axis=0, keepdims=True))
            alpha_ref[h] = jnp.exp2(m_old - m_new)
            m_ref[h] = m_new
        for h in range(N_HEADS_A):
            p_ref[h] = jnp.exp2(s_ref[h] - m_ref[h]).astype(BF16)
            if score and h % SLOTS_PER_SUM == SLOTS_PER_SUM - 1:
                score_slots(j, tuple(range(h + 1 - SLOTS_PER_SUM, h + 1)))
        for h in range(N_HEADS_A):
            acc_ref[h] = alpha_ref[h] * acc_ref[h] + jnp.dot(vt_ref[:, rows], p_ref[h],
                                                             preferred_element_type=F32)
        if score:
            score_finish(j, diagonal=False)

    def is_near(j):
        return pmin_ref[b, i_att] - pmax_ref[b, j] < FAR_REL

    def fused_tile(j, carry):
        near = is_near(j)

        @pl.when(near)
        def _():
            attend_tile(j, True, score=True)

        @pl.when(jnp.logical_not(near))
        def _():
            attend_tile(j, False, score=True)
        return carry

    def attend_only_tile(j, carry):
        near = is_near(j)

        @pl.when(near)
        def _():
            attend_tile(j, True)

        @pl.when(jnp.logical_not(near))
        def _():
            attend_tile(j, False)
        return carry

    @pl.when(scoring & attending)
    def _():
        lax.fori_loop(0, i, fused_tile, 0)

    @pl.when(jnp.logical_not(scoring))
    def _():
        lax.fori_loop(0, i, attend_only_tile, 0)

    @pl.when(scoring)
    def _():
        for p in range(0, IDX_HEADS // 2, SLOTS_PER_SUM):
            score_slots(i, tuple(range(p, p + SLOTS_PER_SUM)))
        score_finish(i, diagonal=True)

    @pl.when(scoring)
    def _():
        n_kt = i + 1

        def fold8(x, op):
            return op(x.reshape(tk // 8, 8, tq), axis=0)

        def count_where(pred):
            def body(j, c):
                hit = pred(j, keys_ref[cur, tile_rows(j), :]).astype(I32)
                return c + jnp.sum(hit.reshape(COUNT_CHAINS, tk // (8 * COUNT_CHAINS), 8, tq), axis=1)
            c8 = lax.fori_loop(0, n_kt, body, jnp.zeros((COUNT_CHAINS, 8, tq), I32))
            return jnp.sum(jnp.sum(c8, axis=0), axis=0, keepdims=True)

        def minmax_body(j, c):
            kt = keys_ref[cur, tile_rows(j), :]
            lo8 = fold8(jnp.where(kt == INT_MIN, INT_MAX, kt), jnp.min)
            return jnp.minimum(c[0], lo8), jnp.maximum(c[1], fold8(kt, jnp.max))
        mn8, mx8 = lax.fori_loop(0, n_kt, minmax_body,
                                 (jnp.full((8, tq), INT_MAX, I32), jnp.full((8, tq), INT_MIN, I32)))
        key_min = jnp.min(mn8, axis=0, keepdims=True)
        key_max = jnp.max(mx8, axis=0, keepdims=True)

        n_causal = i * tq + lax.broadcasted_iota(I32, (1, tq), 1) + 1
        short = n_causal <= top_k
        lo0 = jnp.where(short, INT_MIN + 1, key_min)
        hi0 = jnp.where(short, INT_MIN + 2, jnp.where(key_max == INT_MAX, key_max, key_max + 1))
        cnt0 = jnp.where(short, top_k, n_causal)

        def active(lo, hi, cnt):
            return (cnt != top_k) & (hi > lo + 1)

        def any_lane(mask):
            return jnp.max(jnp.where(mask, 1.0, 0.0)) > 0.0

        def search_cond(c):
            it, lo, hi, cnt, _ = c
            return (it < SEARCH_MAX_STEPS) & any_lane(active(lo, hi, cnt))

        def search_body(c):
            it, lo, hi, cnt, cnt_hi = c
            for u in range(SEARCH_UNROLL):
                mid_key = (lo & hi) + ((lo ^ hi) >> 1)
                v_lo, v_hi = _unsortable(lo), _unsortable(hi)
                above = (cnt - top_k).astype(F32) + 0.5
                frac = 0.25 + 0.5 * above / jnp.maximum(cnt - cnt_hi, 1).astype(F32)
                mid_val = _sortable(v_lo + (v_hi - v_lo) * frac)
                piv = jnp.where(it + u < SEARCH_VALUE_STEPS, mid_val, mid_key)
                piv = jnp.minimum(jnp.maximum(piv, lo + 1), hi - 1)
                c_piv = count_where(lambda j, kt: kt >= piv)
                act = active(lo, hi, cnt)
                up = act & (c_piv >= top_k)
                down = act & (c_piv < top_k)
                lo, cnt = jnp.where(up, piv, lo), jnp.where(up, c_piv, cnt)
                hi, cnt_hi = jnp.where(down, piv, hi), jnp.where(down, c_piv, cnt_hi)
            return it + SEARCH_UNROLL, lo, hi, cnt, cnt_hi
        _, thr, _, cnt_thr, _ = lax.while_loop(search_cond, search_body,
                                               (jnp.int32(0), lo0, hi0, cnt0, jnp.zeros((1, tq), I32)))
        thr_ref[cur] = thr

        tied = cnt_thr > top_k

        @pl.when(any_lane(tied))
        def _():
            need = top_k - count_where(lambda j, kt: kt > thr)
            s_of = lambda j: j * tk + lax.broadcasted_iota(I32, (tk, tq), 0)

            def cut_step(n, c):
                c_lo, c_hi = c
                mid = (c_lo + c_hi) >> 1
                enough = count_where(lambda j, kt: (kt == thr) & (s_of(j) < mid)) >= need
                return jnp.where(enough, c_lo, mid), jnp.where(enough, mid, c_hi)
            _, cut = lax.fori_loop(0, max(1, (seq - 1).bit_length()), cut_step,
                                   (jnp.zeros((1, tq), I32), jnp.full((1, tq), seq, I32)))

            def demote(j, carry):
                kt = keys_ref[cur, tile_rows(j), :]
                drop = tied & (kt == thr) & (s_of(j) >= cut)
                keys_ref[cur, tile_rows(j), :] = jnp.where(drop, INT_MIN, kt)
                return carry
            lax.fori_loop(0, n_kt, demote, 0)

    @pl.when(attending)
    def _():
        for h in range(N_HEADS_A):
            sl = slice(h * HEAD_DIM, (h + 1) * HEAD_DIM)
            acc = acc_ref[h]
            o = (acc[:HEAD_DIM] / acc[HEAD_DIM:HEAD_DIM + 1]).T
            g = g_ref[0, :, sl].astype(F32)
            out_ref[0, :, sl] = (o * (g / (1.0 + jnp.exp(-g)))).astype(BF16)


def _bias_table(rel_bias):
    r = jnp.arange(LANES, dtype=I32)
    relf = jnp.maximum(r, 1).astype(F32)
    large = MAX_EXACT + (jnp.log(relf / MAX_EXACT) / math.log(MAX_DISTANCE / MAX_EXACT)
                         * (N_BUCKETS - MAX_EXACT)).astype(I32)
    bucket = jnp.where(r < MAX_EXACT, r, jnp.minimum(large, N_BUCKETS - 1))
    return ((rel_bias[bucket] - rel_bias[N_BUCKETS - 1][None, :]) * LOG2E).T.astype(F32)


def _sparse_attention(proj3, kw3, positions, tbl, pmin, pmax, q_gain, k_gain, tq):
    bsz, seq, _ = proj3.shape
    top_k = min(TOPK_MAX, seq // 4)
    n_q = seq // tq
    att = lambda b, i, *_: (b, jnp.maximum(i - 1, 0))
    sco = lambda b, i, *_: (b, jnp.minimum(i, n_q - 1))
    wide = lambda seg, tile: pl.BlockSpec((1, tq, 1024), lambda *a: (*tile(*a), seg // 1024))
    full = lambda seg: pl.BlockSpec((1, seq, LANES), lambda b, i, *_: (b, 0, seg // LANES))
    fixed = lambda shape: pl.BlockSpec(shape, lambda b, i, *_: (0,) * len(shape))
    grid_spec = pltpu.PrefetchScalarGridSpec(
        num_scalar_prefetch=2,
        grid=(bsz, n_q + 1),
        in_specs=[wide(SEG_QA, att), wide(SEG_GA, att),
                  pl.BlockSpec((1, 1, tq), lambda b, i, *_: (b, 0, jnp.maximum(i - 1, 0))),
                  wide(SEG_QI, sco),
                  pl.BlockSpec((1, tq, LANES), lambda *a: (*sco(*a), 0)),
                  full(SEG_KA), full(SEG_VA), full(SEG_KIW),
                  pl.BlockSpec((1, seq, 1), lambda b, i, *_: (b, 0, 0)),
                  fixed((N_HEADS_A, LANES)), fixed((1, LANES)), fixed((1, LANES))],
        out_specs=pl.BlockSpec((1, tq, D_A), lambda *a: (*att(*a), 0)),
        scratch_shapes=[pltpu.VMEM((2, seq, tq), I32),
                        pltpu.VMEM((2, 1, tq), I32),
                        pltpu.VMEM((VT_ROWS, seq), BF16),
                        pltpu.VMEM((seq, LANES), BF16),
                        pltpu.VMEM((seq, LANES), BF16),
                        pltpu.VMEM((seq, HEAD_DIM), BF16),
                        pltpu.VMEM((N_HEADS_A, VT_ROWS, tq), F32),
                        pltpu.VMEM((N_HEADS_A, 1, tq), F32),
                        pltpu.VMEM((N_HEADS_A, 1, tq), F32),
                        pltpu.VMEM((IDX_HEADS, tq), F32),
                        pltpu.VMEM((tq, tq), F32),
                        pltpu.VMEM((tq, tq), F32),
                        pltpu.VMEM((N_HEADS_A, tq, tq), F32),
                        pltpu.VMEM((N_HEADS_A, tq, tq), BF16),
                        pltpu.VMEM((N_HEADS_A, HEAD_DIM, tq), BF16),
                        pltpu.VMEM((IDX_HEADS // 2, LANES, tq), BF16)])
    return pl.pallas_call(
        functools.partial(_attn_kernel, tq=tq, top_k=top_k, seq=seq),
        grid_spec=grid_spec,
        out_shape=jax.ShapeDtypeStruct((bsz, seq, D_A), BF16),
        compiler_params=pltpu.CompilerParams(dimension_semantics=("arbitrary", "arbitrary"),
                                             vmem_limit_bytes=VMEM_LIMIT),
        name="sparse_attention",
    )(pmin, pmax, proj3, proj3, positions.reshape(bsz, 1, seq), proj3, kw3, proj3, proj3, proj3,
      positions.reshape(bsz, seq, 1), tbl, q_gain.reshape(1, LANES), k_gain.reshape(1, LANES))


def _ret_kernel(gch_ref, q_ref, k_ref, v_ref, g_ref, dmask_ref, zeta_ref, xi_ref, gain_ref,
                out_ref, state_ref, intra_ref, u_ref, *, tc):
    c = pl.program_id(1)
    chunks = [slice(cc * RET_CHUNK, (cc + 1) * RET_CHUNK) for cc in range(tc // RET_CHUNK)]
    heads = [slice(h * HEAD_DIM, (h + 1) * HEAD_DIM) for h in range(N_HEADS_B)]

    @pl.when(c == 0)
    def _():
        state_ref[...] = jnp.zeros(state_ref.shape, F32)

    for h, sl in enumerate(heads):
        for cc, rows in enumerate(chunks):
            q = q_ref[0, rows, sl]
            k = k_ref[0, rows, sl]
            v = v_ref[0, rows, sl]
            s_in = _nt_dot(q, k) * dmask_ref[h]
            intra_ref[h, cc] = jnp.dot(s_in.astype(BF16), v, preferred_element_type=F32)
            kz_t = (k.astype(F32) * zeta_ref[h]).T.astype(BF16)
            u_ref[h, cc] = jnp.dot(kz_t, v, preferred_element_type=F32)

    for h, sl in enumerate(heads):
        state = state_ref[h]
        for cc, rows in enumerate(chunks):
            cross = jnp.dot(q_ref[0, rows, sl], state.astype(BF16), preferred_element_type=F32) * xi_ref[h]
            o = intra_ref[h, cc] + cross
            state = gch_ref[h] * state + u_ref[h, cc]
            mu = jnp.mean(o, axis=-1, keepdims=True)
            var = jnp.mean(jnp.square(o - mu), axis=-1, keepdims=True)
            y = (o - mu) * lax.rsqrt(var + EPS) * gain_ref[:, sl]
            g = g_ref[0, rows, sl].astype(F32)
            out_ref[0, rows, sl] = (y * (g / (1.0 + jnp.exp(-g)))).astype(BF16)
        state_ref[h] = state


def _retention_tables():
    c = RET_CHUNK
    log_g = jnp.log1p(-jnp.exp2(-5.0 - jnp.arange(N_HEADS_B, dtype=F32)))
    i = jnp.arange(c, dtype=F32)
    diff = i[:, None] - i[None, :]
    dmask = jnp.where(diff >= 0, jnp.exp(log_g[:, None, None] * jnp.maximum(diff, 0.0)), 0.0)
    zeta = jnp.exp(log_g[:, None] * (c - 1.0 - i)[None, :])
    xi = jnp.exp(log_g[:, None] * (i + 1.0)[None, :])
    g_chunk = jnp.exp(log_g * c)
    widen = lambda a: jnp.broadcast_to(a[:, :, None], (N_HEADS_B, c, HEAD_DIM))
    return g_chunk, dmask, widen(zeta), widen(xi)


def _retention(proj3, ret_gain, tables):
    bsz, seq, _ = proj3.shape
    tc = min(512, seq)
    g_chunk, dmask, zeta, xi = tables
    wide = lambda seg: pl.BlockSpec((1, tc, 1024), lambda b, c: (b, c, seg // 1024))
    table = pl.BlockSpec((N_HEADS_B, RET_CHUNK, HEAD_DIM), lambda b, c: (0, 0, 0))
    return pl.pallas_call(
        functools.partial(_ret_kernel, tc=tc),
        grid=(bsz, seq // tc),
        in_specs=[pl.BlockSpec(memory_space=pltpu.SMEM),
                  wide(SEG_QB), wide(SEG_KB), wide(SEG_VB), wide(SEG_GB),
                  table, table, table,
                  pl.BlockSpec((1, D_B), lambda b, c: (0, 0))],
        out_specs=pl.BlockSpec((1, tc, D_B), lambda b, c: (b, c, 0)),
        out_shape=jax.ShapeDtypeStruct((bsz, seq, D_B), BF16),
        scratch_shapes=[pltpu.VMEM((N_HEADS_B, HEAD_DIM, HEAD_DIM), F32),
                        pltpu.VMEM((N_HEADS_B, tc // RET_CHUNK, RET_CHUNK, HEAD_DIM), F32),
                        pltpu.VMEM((N_HEADS_B, tc // RET_CHUNK, HEAD_DIM, HEAD_DIM), F32)],
        compiler_params=pltpu.CompilerParams(dimension_semantics=("arbitrary", "arbitrary"),
                                             vmem_limit_bytes=VMEM_LIMIT),
        name="retention",
    )(g_chunk, proj3, proj3, proj3, proj3, dmask, zeta, xi, ret_gain.reshape(1, D_B))


def _out_kernel(ya_ref, yb_ref, wa_ref, wb_ref, x_ref, gate_ref, out_ref):
    y = jnp.dot(ya_ref[...], wa_ref[0], preferred_element_type=F32)
    y = y + jnp.dot(yb_ref[...], wb_ref[0], preferred_element_type=F32)
    out_ref[...] = x_ref[...] + gate_ref[0] * y


def _out_projection(ya2, yb2, w_out, layer, x2, gate, seq):
    m, d = x2.shape
    tm = min(1024, seq)
    tn = 1024
    per_batch = seq // tm
    bsz = gate.shape[0]
    return pl.pallas_call(
        _out_kernel,
        grid=(m // tm, d // tn),
        in_specs=[pl.BlockSpec((tm, D_A), lambda i, j: (i, 0)),
                  pl.BlockSpec((tm, D_B), lambda i, j: (i, 0)),
                  pl.BlockSpec((1, D_A, tn), lambda i, j: (layer, 0, j)),
                  pl.BlockSpec((1, D_B, tn), lambda i, j: (layer, 1, j)),
                  pl.BlockSpec((tm, tn), lambda i, j: (i, j)),
                  pl.BlockSpec((1, 1, tn), lambda i, j: (i // per_batch, 0, j))],
        out_specs=pl.BlockSpec((tm, tn), lambda i, j: (i, j)),
        out_shape=jax.ShapeDtypeStruct((m, d), F32),
        compiler_params=pltpu.CompilerParams(dimension_semantics=("arbitrary", "arbitrary"),
                                             vmem_limit_bytes=VMEM_LIMIT),
        name="out_proj_residual",
    )(ya2, yb2, w_out, w_out, x2, gate.reshape(bsz, 1, d))


def kernel(x, c, positions, rel_bias, norm_gain, w_mod, b_mod, w_in, q_norm_gain, k_norm_gain,
           ret_norm_gain, w_out):
    bsz, seq, d = x.shape
    depth = w_in.shape[0]
    m = bsz * seq
    tq = min(256, seq)
    assert d == D_A + D_B and seq % tq == 0 and seq % RET_CHUNK == 0

    mod = _modulation(c, w_mod, b_mod)
    cs = _rope_tables(positions)
    tbl = _bias_table(rel_bias)
    pos_tiles = positions.reshape(bsz, seq // tq, tq)
    pmin = jnp.min(pos_tiles, axis=-1).astype(I32)
    pmax = jnp.max(pos_tiles, axis=-1).astype(I32)
    tables = _retention_tables()
    w_in_packed = _pack_w_in(w_in)
    w_out_bf16 = w_out.astype(BF16)

    x2 = x.reshape(m, d)
    for l in range(depth):
        shift, scale, gate = mod[l, :, :d], mod[l, :, d:2 * d], mod[l, :, 2 * d:]
        proj, kw = _in_projection(x2, norm_gain[l], scale, shift, w_in_packed, l, cs, seq)
        proj3 = proj.reshape(bsz, seq, N_PACK)
        y_a = _sparse_attention(proj3, kw.reshape(bsz, seq, LANES), positions, tbl, pmin, pmax,
                                q_norm_gain[l], k_norm_gain[l], tq)
        y_b = _retention(proj3, ret_norm_gain[l], tables)
        x2 = _out_projection(y_a.reshape(m, D_A), y_b.reshape(m, D_B), w_out_bf16, l, x2, gate, seq)
    return x2.reshape(bsz, seq, d)
```

```python
import functools
import math

import numpy as np
import jax
import jax.numpy as jnp
from jax import lax
from jax.experimental import pallas as pl
from jax.experimental.pallas import tpu as pltpu

F32 = jnp.float32
BF16 = jnp.bfloat16
I32 = jnp.int32

HEAD_DIM = 128
N_HEADS_A = 8
N_HEADS_B = 8
D_A = N_HEADS_A * HEAD_DIM
D_B = N_HEADS_B * HEAD_DIM
IDX_HEADS = 16
IDX_DIM = 64
TOPK_MAX = 256
RET_CHUNK = 128
N_BUCKETS = 32
MAX_EXACT = 16
MAX_DISTANCE = 128
ROPE_BASE = 10000.0
EPS = 1e-6
IN_SIZES = (D_A, HEAD_DIM, HEAD_DIM, D_A, IDX_HEADS * IDX_DIM, IDX_DIM, IDX_HEADS, D_B, D_B, D_B, D_B)

LANES = 128
VMEM_LIMIT = 56 * 1024 * 1024

SEG_QA, SEG_GA, SEG_QI, SEG_QB, SEG_KB, SEG_VB, SEG_GB = (n * 1024 for n in range(7))
SEG_KA = 7 * 1024
SEG_VA = SEG_KA + LANES
SEG_KIW = SEG_VA + LANES
N_PACK = 7 * 1024 + 512
PROJ_TN = 1536
PROJ_SUB = 512
PROJ_ROWS = 256

INT_MIN = np.int32(-2 ** 31)
INT_MAX = np.int32(2 ** 31 - 1)
LOG2E = math.log2(math.e)
NEG_BIG = -1e30
VT_ROWS = HEAD_DIM + 16
FAR_REL = 113
COUNT_CHAINS = 4
SEARCH_UNROLL = 3
SEARCH_VALUE_STEPS = 12
SEARCH_MAX_STEPS = SEARCH_VALUE_STEPS + 33


def _sortable(x):
    bits = pltpu.bitcast(x, I32)
    return bits ^ ((bits >> 31) & np.int32(0x7FFFFFFF))


def _unsortable(key):
    return pltpu.bitcast(key ^ ((key >> 31) & np.int32(0x7FFFFFFF)), F32)


def _nt_dot(a, b):
    return lax.dot_general(a, b, (((1,), (1,)), ((), ())), preferred_element_type=F32)


def _mod_kernel(c_ref, w_ref, b_ref, out_ref):
    c = c_ref[...]
    c_act = c * (1.0 / (1.0 + jnp.exp(-c)))
    out_ref[0] = jnp.dot(c_act, w_ref[0], preferred_element_type=F32) + b_ref[0]


def _modulation(c, w_mod, b_mod):
    depth, d, n3 = w_mod.shape
    b = c.shape[0]
    bp = max(8, b)
    c_pad = jnp.zeros((bp, d), F32).at[:b].set(c)
    tn = 768
    out = pl.pallas_call(
        _mod_kernel,
        grid=(depth, n3 // tn),
        in_specs=[pl.BlockSpec((bp, d), lambda l, j: (0, 0)),
                  pl.BlockSpec((1, d, tn), lambda l, j: (l, 0, j)),
                  pl.BlockSpec((1, 1, tn), lambda l, j: (l, 0, j))],
        out_specs=pl.BlockSpec((1, bp, tn), lambda l, j: (l, 0, j)),
        out_shape=jax.ShapeDtypeStruct((depth, bp, n3), F32),
        compiler_params=pltpu.CompilerParams(dimension_semantics=("arbitrary", "arbitrary"),
                                             vmem_limit_bytes=VMEM_LIMIT),
        name="adaln_mod",
    )(c_pad, w_mod, b_mod.reshape(depth, 1, n3))
    return out[:, :b]


def _rope_kernel(pos_ref, inv_ref, sign_ref, out_ref):
    ang = pos_ref[...].astype(F32) * inv_ref[...]
    out_ref[:, :LANES] = jnp.cos(ang)
    out_ref[:, LANES:] = jnp.sin(ang) * sign_ref[...]


def _rope_tables(positions):
    m = positions.size
    half = HEAD_DIM // 2
    inv = ROPE_BASE ** (-jnp.arange(half, dtype=F32) / half)
    inv2 = jnp.concatenate([inv, inv]).reshape(1, LANES)
    sign = jnp.concatenate([-jnp.ones((half,), F32), jnp.ones((half,), F32)]).reshape(1, LANES)
    tm = min(2048, m)
    return pl.pallas_call(
        _rope_kernel,
        grid=(m // tm,),
        in_specs=[pl.BlockSpec((tm, 1), lambda i: (i, 0)),
                  pl.BlockSpec((1, LANES), lambda i: (0, 0)),
                  pl.BlockSpec((1, LANES), lambda i: (0, 0))],
        out_specs=pl.BlockSpec((tm, 2 * LANES), lambda i: (i, 0)),
        out_shape=jax.ShapeDtypeStruct((m, 2 * LANES), F32),
        compiler_params=pltpu.CompilerParams(dimension_semantics=("arbitrary",)),
        name="rope_tables",
    )(positions.reshape(m, 1), inv2, sign)


def _group_rmsnorm(seg, gain):
    ms = jnp.mean(seg * seg, axis=-1, keepdims=True)
    return seg * lax.rsqrt(ms + EPS) * gain


def _proj_kernel(x_ref, g_ref, scale_ref, shift_ref, w_ref, cs_ref, out_ref, kw_ref, h_ref):
    j = pl.program_id(1)
    subs = PROJ_TN // PROJ_SUB
    kiw_sub, kiw_lo = divmod(SEG_KIW % PROJ_TN, PROJ_SUB)

    def project(rows, h):
        for c in range(subs):
            sub = j * subs + c
            is_rot = (sub >= SEG_QB // PROJ_SUB) & (sub < SEG_VB // PROJ_SUB)
            rot_scale = jnp.where(sub >= SEG_KB // PROJ_SUB, HEAD_DIM ** -0.5, 1.0)
            acc = jnp.dot(h, w_ref[0, :, c * PROJ_SUB:(c + 1) * PROJ_SUB], preferred_element_type=F32)
            a_mul = jnp.where(is_rot, cs_ref[rows, :LANES] * rot_scale, 1.0)
            b_mul = jnp.where(is_rot, cs_ref[rows, LANES:] * rot_scale, 0.0)
            for a in range(PROJ_SUB // LANES):
                seg = acc[:, a * LANES:(a + 1) * LANES]
                col = c * PROJ_SUB + a * LANES
                out_ref[rows, col:col + LANES] = (
                    seg * a_mul + pltpu.roll(seg, HEAD_DIM // 2, axis=1) * b_mul).astype(BF16)
            if c == kiw_sub:
                kw_ref[rows, :] = acc[:, kiw_lo:kiw_lo + LANES]

    chunks = [slice(r * PROJ_ROWS, (r + 1) * PROJ_ROWS) for r in range(h_ref.shape[0] // PROJ_ROWS)]

    @pl.when(j == 0)
    def _():
        for rows in chunks:
            x = x_ref[rows, :]
            ms = jnp.mean(x * x, axis=-1, keepdims=True)
            y = x * lax.rsqrt(ms + EPS) * g_ref[...]
            h = (y * (1.0 + scale_ref[0]) + shift_ref[0]).astype(BF16)
            h_ref[rows, :] = h
            project(rows, h)

    @pl.when(j != 0)
    def _():
        for rows in chunks:
            project(rows, h_ref[rows, :])


def _pack_w_in(w):
    splits = [int(p) for p in np.cumsum(IN_SIZES)[:-1]]
    q_a, k_a, v_a, g_a, q_i, k_i, w_i, q_b, k_b, v_b, g_b = jnp.split(w, splits, axis=-1)
    pad = jnp.zeros(w.shape[:-1] + (N_PACK - SEG_KIW - IDX_DIM - IDX_HEADS,), w.dtype)
    parts = [q_a, g_a, q_i, q_b, k_b, v_b, g_b, k_a, v_a, k_i, w_i, pad]
    return jnp.concatenate([p.astype(BF16) for p in parts], axis=-1)


def _in_projection(x2, norm_gain, scale, shift, w_pack, layer, cs, seq):
    m, d = x2.shape
    tm = min(1024, seq)
    per_batch = seq // tm
    bsz = scale.shape[0]
    return pl.pallas_call(
        _proj_kernel,
        grid=(m // tm, N_PACK // PROJ_TN),
        in_specs=[pl.BlockSpec((tm, d), lambda i, j: (i, 0)),
                  pl.BlockSpec((1, d), lambda i, j: (0, 0)),
                  pl.BlockSpec((1, 1, d), lambda i, j: (i // per_batch, 0, 0)),
                  pl.BlockSpec((1, 1, d), lambda i, j: (i // per_batch, 0, 0)),
                  pl.BlockSpec((1, d, PROJ_TN), lambda i, j: (layer, 0, j)),
                  pl.BlockSpec((tm, 2 * LANES), lambda i, j: (i, 0))],
        out_specs=[pl.BlockSpec((tm, PROJ_TN), lambda i, j: (i, j)),
                   pl.BlockSpec((tm, LANES), lambda i, j: (i, 0))],
        out_shape=[jax.ShapeDtypeStruct((m, N_PACK), BF16),
                   jax.ShapeDtypeStruct((m, LANES), F32)],
        scratch_shapes=[pltpu.VMEM((tm, d), BF16)],
        compiler_params=pltpu.CompilerParams(dimension_semantics=("arbitrary", "arbitrary"),
                                             vmem_limit_bytes=VMEM_LIMIT),
        name="norm_in_proj",
    )(x2, norm_gain.reshape(1, d), scale.reshape(bsz, 1, d), shift.reshape(bsz, 1, d), w_pack, cs)


def _attn_kernel(pmin_ref, pmax_ref,
                 q_ref, g_ref, posr_ref, qi_ref, kwq_ref,
                 k_ref, v_ref, kiw_ref, posc_ref, tbl_ref, qg_ref, kg_ref,
                 out_ref,
                 keys_ref, thr_ref, vt_ref, klo_ref, khi_ref, kn_ref, acc_ref, m_ref, alpha_ref,
                 wt_ref, pen_ref, sc_ref, s_ref, p_ref, qt_ref, qit_ref, kmin_ref, kmax_ref,
                 *, tq, top_k, seq):
    b = pl.program_id(0)
    i = pl.program_id(1)
    n_q = seq // tq
    tk = tq
    cur = i % 2
    prv = 1 - cur
    i_att = i - 1
    scoring = i < n_q
    attending = i >= 1
    tiles_per_trip = 2 if n_q % 2 == 0 else 1

    def tile_rows(j):
        return pl.ds(pl.multiple_of(j * tk, tk), tk)

    @pl.when(i == 0)
    def _():
        def prep(c, carry):
            rows = tile_rows(c)
            vt_ref[:HEAD_DIM, rows] = v_ref[0, rows, :].astype(F32).T.astype(BF16)
            row = lax.broadcasted_iota(I32, (VT_ROWS - HEAD_DIM, tk), 0)
            vt_ref[HEAD_DIM:, rows] = jnp.where(row == 0, 1.0, 0.0).astype(BF16)
            kn_ref[rows, :] = _group_rmsnorm(k_ref[0, rows, :].astype(F32), kg_ref[...]).astype(BF16)
            kf = kiw_ref[0, rows, :].astype(F32)
            lane = lax.broadcasted_iota(I32, kf.shape, 1)
            klo_ref[rows, :] = jnp.where(lane < IDX_DIM, kf, 0.0).astype(BF16)
            khi_ref[rows, :] = jnp.where(lane >= IDX_DIM, pltpu.roll(kf, IDX_DIM, axis=1), 0.0).astype(BF16)
            return carry
        lax.fori_loop(0, seq // tk, prep, 0)

    @pl.when(scoring)
    def _():
        wt_ref[...] = kwq_ref[0].T[IDX_DIM:IDX_DIM + IDX_HEADS, :]
        for p in range(IDX_HEADS // 2):
            qit_ref[p] = qi_ref[0, :, p * LANES:(p + 1) * LANES].astype(F32).T.astype(BF16)
        kmin_ref[...] = jnp.full(kmin_ref.shape, INT_MAX, I32)
        kmax_ref[...] = jnp.full(kmax_ref.shape, INT_MIN, I32)

    @pl.when(attending)
    def _():
        q_gain = qg_ref[...] * (HEAD_DIM ** -0.5 * LOG2E)
        for h in range(N_HEADS_A):
            qn = _group_rmsnorm(q_ref[0, :, h * HEAD_DIM:(h + 1) * HEAD_DIM].astype(F32), q_gain)
            qt_ref[h] = qn.T.astype(BF16)
        m_ref[...] = jnp.full(m_ref.shape, NEG_BIG, F32)
        acc_ref[...] = jnp.zeros(acc_ref.shape, F32)


    def score_slots(j, slots):
        rows = tile_rows(j)
        term = None
        for p in slots:
            d_lo = jnp.dot(klo_ref[rows, :], qit_ref[p], preferred_element_type=F32)
            d_hi = jnp.dot(khi_ref[rows, :], qit_ref[p], preferred_element_type=F32)
            t = (wt_ref[2 * p:2 * p + 1, :] * jnp.maximum(d_lo, 0.0)
                 + wt_ref[2 * p + 1:2 * p + 2, :] * jnp.maximum(d_hi, 0.0))
            term = t if term is None else term + t
        sc_ref[...] = term if slots[0] == 0 else sc_ref[...] + term

    def score_finish(j, diagonal):
        key = _sortable(sc_ref[...])
        low = key
        if diagonal:
            causal = (lax.broadcasted_iota(I32, (tk, tq), 0) <= lax.broadcasted_iota(I32, (tk, tq), 1))
            key = jnp.where(causal, key, INT_MIN)
            low = jnp.where(causal, low, INT_MAX)
        keys_ref[cur, tile_rows(j), :] = key
        kmin_ref[...] = jnp.minimum(kmin_ref[...], jnp.min(low.reshape(tk // 8, 8, tq), axis=0))
        kmax_ref[...] = jnp.maximum(kmax_ref[...], jnp.max(key.reshape(tk // 8, 8, tq), axis=0))

    def attend_tile(j, with_bias, score=False):
        rows = tile_rows(j)
        kt = kn_ref[rows, :]
        pen_ref[...] = jnp.where(keys_ref[prv, rows, :] >= thr_ref[prv], 0.0, NEG_BIG)
        if with_bias:
            rel = jnp.clip(posr_ref[0] - posc_ref[0, rows, :], 0, LANES - 1)
        for h in range(N_HEADS_A):
            s = jnp.dot(kt, qt_ref[h], preferred_element_type=F32) + pen_ref[...]
            if with_bias:
                tb = jnp.broadcast_to(tbl_ref[h:h + 1, :], (tk, LANES))
                bias = [jnp.take_along_axis(tb, rel[:, c * LANES:(c + 1) * LANES], axis=1)
                        for c in range(tq // LANES)]
                s = s + jnp.concatenate(bias, axis=1)
            s_ref[h] = s
            m_old = m_ref[h]
            m_new = jnp.maximum(m_old, jnp.max(s_ref[h], axis=0, keepdims=True))
            alpha_ref[h] = jnp.exp2(m_old - m_new)
            m_ref[h] = m_new
        for h in range(N_HEADS_A):
            p_ref[h] = jnp.exp2(s_ref[h] - m_ref[h]).astype(BF16)
            if score:
                score_slots(j, (h,))
        for h in range(N_HEADS_A):
            acc_ref[h] = alpha_ref[h] * acc_ref[h] + jnp.dot(vt_ref[:, rows], p_ref[h],
                                                             preferred_element_type=F32)
        if score:
            score_finish(j, diagonal=False)

    def is_near(j):
        return pmin_ref[b, i_att] - pmax_ref[b, j] < FAR_REL

    def fused_tile(j, carry):
        near = is_near(j)

        @pl.when(near)
        def _():
            attend_tile(j, True, score=True)

        @pl.when(jnp.logical_not(near))
        def _():
            attend_tile(j, False, score=True)
        return carry

    def attend_only_tile(j, carry):
        near = is_near(j)

        @pl.when(near)
        def _():
            attend_tile(j, True)

        @pl.when(jnp.logical_not(near))
        def _():
            attend_tile(j, False)
        return carry

    @pl.when(scoring & attending)
    def _():
        lax.fori_loop(0, i, fused_tile, 0)

    @pl.when(jnp.logical_not(scoring))
    def _():
        lax.fori_loop(0, i, attend_only_tile, 0)

    @pl.when(scoring)
    def _():
        for p in range(IDX_HEADS // 2):
            score_slots(i, (p,))
        score_finish(i, diagonal=True)
        if tiles_per_trip == 2:
            @pl.when(i % 2 == 0)
            def _():
                keys_ref[cur, tile_rows(i + 1), :] = jnp.full((tk, tq), INT_MIN, I32)

    @pl.when(scoring)
    def _():
        n_kt = i + 1

        def count_where(pred):
            def body(jj, c):
                for t in range(tiles_per_trip):
                    j = jj * tiles_per_trip + t
                    hit = pred(j, keys_ref[cur, tile_rows(j), :]).astype(I32)
                    c = c + jnp.sum(hit.reshape(COUNT_CHAINS, tk // (8 * COUNT_CHAINS), 8, tq), axis=1)
                return c
            c8 = lax.fori_loop(0, (n_kt + tiles_per_trip - 1) // tiles_per_trip, body,
                               jnp.zeros((COUNT_CHAINS, 8, tq), I32))
            return jnp.sum(jnp.sum(c8, axis=0), axis=0, keepdims=True)

        key_min = jnp.min(kmin_ref[...], axis=0, keepdims=True)
        key_max = jnp.max(kmax_ref[...], axis=0, keepdims=True)

        n_causal = i * tq + lax.broadcasted_iota(I32, (1, tq), 1) + 1
        short = n_causal <= top_k
        lo0 = jnp.where(short, INT_MIN + 1, key_min)
        hi0 = jnp.where(short, INT_MIN + 2, jnp.where(key_max == INT_MAX, key_max, key_max + 1))
        cnt0 = jnp.where(short, top_k, n_causal)

        def active(lo, hi, cnt):
            return (cnt != top_k) & (hi > lo + 1)

        def any_lane(mask):
            return jnp.max(jnp.where(mask, 1.0, 0.0)) > 0.0

        def search_cond(c):
            it, lo, hi, cnt, _ = c
            return (it < SEARCH_MAX_STEPS) & any_lane(active(lo, hi, cnt))

        def search_body(c):
            it, lo, hi, cnt, cnt_hi = c
            for u in range(SEARCH_UNROLL):
                mid_key = (lo & hi) + ((lo ^ hi) >> 1)
                v_lo, v_hi = _unsortable(lo), _unsortable(hi)
                above = (cnt - top_k).astype(F32) + 0.5
                frac = 0.25 + 0.5 * above / jnp.maximum(cnt - cnt_hi, 1).astype(F32)
                mid_val = _sortable(v_lo + (v_hi - v_lo) * frac)
                piv = jnp.where(it + u < SEARCH_VALUE_STEPS, mid_val, mid_key)
                piv = jnp.minimum(jnp.maximum(piv, lo + 1), hi - 1)
                c_piv = count_where(lambda j, kt: kt >= piv)
                act = active(lo, hi, cnt)
                up = act & (c_piv >= top_k)
                down = act & (c_piv < top_k)
                lo, cnt = jnp.where(up, piv, lo), jnp.where(up, c_piv, cnt)
                hi, cnt_hi = jnp.where(down, piv, hi), jnp.where(down, c_piv, cnt_hi)
            return it + SEARCH_UNROLL, lo, hi, cnt, cnt_hi
        _, thr, _, cnt_thr, _ = lax.while_loop(search_cond, search_body,
                                               (jnp.int32(0), lo0, hi0, cnt0, jnp.zeros((1, tq), I32)))
        thr_ref[cur] = thr

        tied = cnt_thr > top_k

        @pl.when(any_lane(tied))
        def _():
            need = top_k - count_where(lambda j, kt: kt > thr)
            s_of = lambda j: j * tk + lax.broadcasted_iota(I32, (tk, tq), 0)

            def cut_step(n, c):
                c_lo, c_hi = c
                mid = (c_lo + c_hi) >> 1
                enough = count_where(lambda j, kt: (kt == thr) & (s_of(j) < mid)) >= need
                return jnp.where(enough, c_lo, mid), jnp.where(enough, mid, c_hi)
            _, cut = lax.fori_loop(0, max(1, (seq - 1).bit_length()), cut_step,
                                   (jnp.zeros((1, tq), I32), jnp.full((1, tq), seq, I32)))

            def demote(j, carry):
                kt = keys_ref[cur, tile_rows(j), :]
                drop = tied & (kt == thr) & (s_of(j) >= cut)
                keys_ref[cur, tile_rows(j), :] = jnp.where(drop, INT_MIN, kt)
                return carry
            lax.fori_loop(0, n_kt, demote, 0)

    @pl.when(attending)
    def _():
        for h in range(N_HEADS_A):
            sl = slice(h * HEAD_DIM, (h + 1) * HEAD_DIM)
            acc = acc_ref[h]
            o = (acc[:HEAD_DIM] / acc[HEAD_DIM:HEAD_DIM + 1]).T
            g = g_ref[0, :, sl].astype(F32)
            out_ref[0, :, sl] = (o * (g / (1.0 + jnp.exp(-g)))).astype(BF16)


def _bias_table(rel_bias):
    r = jnp.arange(LANES, dtype=I32)
    relf = jnp.maximum(r, 1).astype(F32)
    large = MAX_EXACT + (jnp.log(relf / MAX_EXACT) / math.log(MAX_DISTANCE / MAX_EXACT)
                         * (N_BUCKETS - MAX_EXACT)).astype(I32)
    bucket = jnp.where(r < MAX_EXACT, r, jnp.minimum(large, N_BUCKETS - 1))
    return ((rel_bias[bucket] - rel_bias[N_BUCKETS - 1][None, :]) * LOG2E).T.astype(F32)


def _sparse_attention(proj3, kw3, positions, tbl, pmin, pmax, q_gain, k_gain, tq):
    bsz, seq, _ = proj3.shape
    top_k = min(TOPK_MAX, seq // 4)
    n_q = seq // tq
    att = lambda b, i, *_: (b, jnp.maximum(i - 1, 0))
    sco = lambda b, i, *_: (b, jnp.minimum(i, n_q - 1))
    wide = lambda seg, tile: pl.BlockSpec((1, tq, 1024), lambda *a: (*tile(*a), seg // 1024))
    full = lambda seg: pl.BlockSpec((1, seq, LANES), lambda b, i, *_: (b, 0, seg // LANES))
    fixed = lambda shape: pl.BlockSpec(shape, lambda b, i, *_: (0,) * len(shape))
    grid_spec = pltpu.PrefetchScalarGridSpec(
        num_scalar_prefetch=2,
        grid=(bsz, n_q + 1),
        in_specs=[wide(SEG_QA, att), wide(SEG_GA, att),
                  pl.BlockSpec((1, 1, tq), lambda b, i, *_: (b, 0, jnp.maximum(i - 1, 0))),
                  wide(SEG_QI, sco),
                  pl.BlockSpec((1, tq, LANES), lambda *a: (*sco(*a), 0)),
                  full(SEG_KA), full(SEG_VA), full(SEG_KIW),
                  pl.BlockSpec((1, seq, 1), lambda b, i, *_: (b, 0, 0)),
                  fixed((N_HEADS_A, LANES)), fixed((1, LANES)), fixed((1, LANES))],
        out_specs=pl.BlockSpec((1, tq, D_A), lambda *a: (*att(*a), 0)),
        scratch_shapes=[pltpu.VMEM((2, seq, tq), I32),
                        pltpu.VMEM((2, 1, tq), I32),
                        pltpu.VMEM((VT_ROWS, seq), BF16),
                        pltpu.VMEM((seq, LANES), BF16),
                        pltpu.VMEM((seq, LANES), BF16),
                        pltpu.VMEM((seq, HEAD_DIM), BF16),
                        pltpu.VMEM((N_HEADS_A, VT_ROWS, tq), F32),
                        pltpu.VMEM((N_HEADS_A, 1, tq), F32),
                        pltpu.VMEM((N_HEADS_A, 1, tq), F32),
                        pltpu.VMEM((IDX_HEADS, tq), F32),
                        pltpu.VMEM((tq, tq), F32),
                        pltpu.VMEM((tq, tq), F32),
                        pltpu.VMEM((N_HEADS_A, tq, tq), F32),
                        pltpu.VMEM((N_HEADS_A, tq, tq), BF16),
                        pltpu.VMEM((N_HEADS_A, HEAD_DIM, tq), BF16),
                        pltpu.VMEM((IDX_HEADS // 2, LANES, tq), BF16),
                        pltpu.VMEM((8, tq), I32),
                        pltpu.VMEM((8, tq), I32)])
    return pl.pallas_call(
        functools.partial(_attn_kernel, tq=tq, top_k=top_k, seq=seq),
        grid_spec=grid_spec,
        out_shape=jax.ShapeDtypeStruct((bsz, seq, D_A), BF16),
        compiler_params=pltpu.CompilerParams(dimension_semantics=("arbitrary", "arbitrary"),
                                             vmem_limit_bytes=VMEM_LIMIT),
        name="sparse_attention",
    )(pmin, pmax, proj3, proj3, positions.reshape(bsz, 1, seq), proj3, kw3, proj3, proj3, proj3,
      positions.reshape(bsz, seq, 1), tbl, q_gain.reshape(1, LANES), k_gain.reshape(1, LANES))


def _ret_kernel(gch_ref, q_ref, k_ref, v_ref, g_ref, dmask_ref, zeta_ref, xi_ref, gain_ref,
                out_ref, state_ref, intra_ref, u_ref, *, tc):
    c = pl.program_id(1)
    chunks = [slice(cc * RET_CHUNK, (cc + 1) * RET_CHUNK) for cc in range(tc // RET_CHUNK)]
    heads = [slice(h * HEAD_DIM, (h + 1) * HEAD_DIM) for h in range(N_HEADS_B)]

    @pl.when(c == 0)
    def _():
        state_ref[...] = jnp.zeros(state_ref.shape, F32)

    for h, sl in enumerate(heads):
        for cc, rows in enumerate(chunks):
            q = q_ref[0, rows, sl]
            k = k_ref[0, rows, sl]
            v = v_ref[0, rows, sl]
            s_in = _nt_dot(q, k) * dmask_ref[h]
            intra_ref[h, cc] = jnp.dot(s_in.astype(BF16), v, preferred_element_type=F32)
            kz_t = (k.astype(F32) * zeta_ref[h]).T.astype(BF16)
            u_ref[h, cc] = jnp.dot(kz_t, v, preferred_element_type=F32)

    for h, sl in enumerate(heads):
        state = state_ref[h]
        for cc, rows in enumerate(chunks):
            cross = jnp.dot(q_ref[0, rows, sl], state.astype(BF16), preferred_element_type=F32) * xi_ref[h]
            o = intra_ref[h, cc] + cross
            state = gch_ref[h] * state + u_ref[h, cc]
            mu = jnp.mean(o, axis=-1, keepdims=True)
            var = jnp.mean(jnp.square(o - mu), axis=-1, keepdims=True)
            y = (o - mu) * lax.rsqrt(var + EPS) * gain_ref[:, sl]
            g = g_ref[0, rows, sl].astype(F32)
            out_ref[0, rows, sl] = (y * (g / (1.0 + jnp.exp(-g)))).astype(BF16)
        state_ref[h] = state


def _retention_tables():
    c = RET_CHUNK
    log_g = jnp.log1p(-jnp.exp2(-5.0 - jnp.arange(N_HEADS_B, dtype=F32)))
    i = jnp.arange(c, dtype=F32)
    diff = i[:, None] - i[None, :]
    dmask = jnp.where(diff >= 0, jnp.exp(log_g[:, None, None] * jnp.maximum(diff, 0.0)), 0.0)
    zeta = jnp.exp(log_g[:, None] * (c - 1.0 - i)[None, :])
    xi = jnp.exp(log_g[:, None] * (i + 1.0)[None, :])
    g_chunk = jnp.exp(log_g * c)
    widen = lambda a: jnp.broadcast_to(a[:, :, None], (N_HEADS_B, c, HEAD_DIM))
    return g_chunk, dmask, widen(zeta), widen(xi)


def _retention(proj3, ret_gain, tables):
    bsz, seq, _ = proj3.shape
    tc = min(512, seq)
    g_chunk, dmask, zeta, xi = tables
    wide = lambda seg: pl.BlockSpec((1, tc, 1024), lambda b, c: (b, c, seg // 1024))
    table = pl.BlockSpec((N_HEADS_B, RET_CHUNK, HEAD_DIM), lambda b, c: (0, 0, 0))
    return pl.pallas_call(
        functools.partial(_ret_kernel, tc=tc),
        grid=(bsz, seq // tc),
        in_specs=[pl.BlockSpec(memory_space=pltpu.SMEM),
                  wide(SEG_QB), wide(SEG_KB), wide(SEG_VB), wide(SEG_GB),
                  table, table, table,
                  pl.BlockSpec((1, D_B), lambda b, c: (0, 0))],
        out_specs=pl.BlockSpec((1, tc, D_B), lambda b, c: (b, c, 0)),
        out_shape=jax.ShapeDtypeStruct((bsz, seq, D_B), BF16),
        scratch_shapes=[pltpu.VMEM((N_HEADS_B, HEAD_DIM, HEAD_DIM), F32),
                        pltpu.VMEM((N_HEADS_B, tc // RET_CHUNK, RET_CHUNK, HEAD_DIM), F32),
                        pltpu.VMEM((N_HEADS_B, tc // RET_CHUNK, HEAD_DIM, HEAD_DIM), F32)],
        compiler_params=pltpu.CompilerParams(dimension_semantics=("arbitrary", "arbitrary"),
                                             vmem_limit_bytes=VMEM_LIMIT),
        name="retention",
    )(g_chunk, proj3, proj3, proj3, proj3, dmask, zeta, xi, ret_gain.reshape(1, D_B))


def _out_kernel(ya_ref, yb_ref, wa_ref, wb_ref, x_ref, gate_ref, out_ref):
    y = jnp.dot(ya_ref[...], wa_ref[0], preferred_element_type=F32)
    y = y + jnp.dot(yb_ref[...], wb_ref[0], preferred_element_type=F32)
    out_ref[...] = x_ref[...] + gate_ref[0] * y


def _out_projection(ya2, yb2, w_out, layer, x2, gate, seq):
    m, d = x2.shape
    tm = min(1024, seq)
    tn = 1024
    per_batch = seq // tm
    bsz = gate.shape[0]
    return pl.pallas_call(
        _out_kernel,
        grid=(m // tm, d // tn),
        in_specs=[pl.BlockSpec((tm, D_A), lambda i, j: (i, 0)),
                  pl.BlockSpec((tm, D_B), lambda i, j: (i, 0)),
                  pl.BlockSpec((1, D_A, tn), lambda i, j: (layer, 0, j)),
                  pl.BlockSpec((1, D_B, tn), lambda i, j: (layer, 1, j)),
                  pl.BlockSpec((tm, tn), lambda i, j: (i, j)),
                  pl.BlockSpec((1, 1, tn), lambda i, j: (i // per_batch, 0, j))],
        out_specs=pl.BlockSpec((tm, tn), lambda i, j: (i, j)),
        out_shape=jax.ShapeDtypeStruct((m, d), F32),
        compiler_params=pltpu.CompilerParams(dimension_semantics=("arbitrary", "arbitrary"),
                                             vmem_limit_bytes=VMEM_LIMIT),
        name="out_proj_residual",
    )(ya2, yb2, w_out, w_out, x2, gate.reshape(bsz, 1, d))


def kernel(x, c, positions, rel_bias, norm_gain, w_mod, b_mod, w_in, q_norm_gain, k_norm_gain,
           ret_norm_gain, w_out):
    bsz, seq, d = x.shape
    depth = w_in.shape[0]
    m = bsz * seq
    tq = min(256, seq)
    assert d == D_A + D_B and seq % tq == 0 and seq % RET_CHUNK == 0

    mod = _modulation(c, w_mod, b_mod)
    cs = _rope_tables(positions)
    tbl = _bias_table(rel_bias)
    pos_tiles = positions.reshape(bsz, seq // tq, tq)
    pmin = jnp.min(pos_tiles, axis=-1).astype(I32)
    pmax = jnp.max(pos_tiles, axis=-1).astype(I32)
    tables = _retention_tables()
    w_in_packed = _pack_w_in(w_in)
    w_out_bf16 = w_out.astype(BF16)

    x2 = x.reshape(m, d)
    for l in range(depth):
        shift, scale, gate = mod[l, :, :d], mod[l, :, d:2 * d], mod[l, :, 2 * d:]
        proj, kw = _in_projection(x2, norm_gain[l], scale, shift, w_in_packed, l, cs, seq)
        proj3 = proj.reshape(bsz, seq, N_PACK)
        y_a = _sparse_attention(proj3, kw.reshape(bsz, seq, LANES), positions, tbl, pmin, pmax,
                                q_norm_gain[l], k_norm_gain[l], tq)
        y_b = _retention(proj3, ret_norm_gain[l], tables)
        x2 = _out_projection(y_a.reshape(m, D_A), y_b.reshape(m, D_B), w_out_bf16, l, x2, gate, seq)
    return x2.reshape(bsz, seq, d)
```

```python
import functools
import math

import numpy as np
import jax
import jax.numpy as jnp
from jax import lax
from jax.experimental import pallas as pl
from jax.experimental.pallas import tpu as pltpu

F32 = jnp.float32
BF16 = jnp.bfloat16
I32 = jnp.int32

HEAD_DIM = 128
N_HEADS_A = 8
N_HEADS_B = 8
D_A = N_HEADS_A * HEAD_DIM
D_B = N_HEADS_B * HEAD_DIM
IDX_HEADS = 16
IDX_DIM = 64
TOPK_MAX = 256
RET_CHUNK = 128
N_BUCKETS = 32
MAX_EXACT = 16
MAX_DISTANCE = 128
ROPE_BASE = 10000.0
EPS = 1e-6
IN_SIZES = (D_A, HEAD_DIM, HEAD_DIM, D_A, IDX_HEADS * IDX_DIM, IDX_DIM, IDX_HEADS, D_B, D_B, D_B, D_B)

LANES = 128
VMEM_LIMIT = 56 * 1024 * 1024

SEG_QA, SEG_GA, SEG_QI, SEG_QB, SEG_KB, SEG_VB, SEG_GB = (n * 1024 for n in range(7))
SEG_KA = 7 * 1024
SEG_VA = SEG_KA + LANES
SEG_KIW = SEG_VA + LANES
N_PACK = 7 * 1024 + 512
PROJ_TN = 1536
PROJ_SUB = 512
PROJ_ROWS = 256

INT_MIN = np.int32(-2 ** 31)
INT_MAX = np.int32(2 ** 31 - 1)
LOG2E = math.log2(math.e)
NEG_BIG = -1e30
VT_ROWS = HEAD_DIM + 16
FAR_REL = 113
COUNT_CHAINS = 4
SEARCH_UNROLL = 3
SEARCH_VALUE_STEPS = 12
SEARCH_INTERP_WEIGHT = 0.7
SEARCH_MAX_STEPS = SEARCH_VALUE_STEPS + 33


def _sortable(x):
    bits = pltpu.bitcast(x, I32)
    return bits ^ ((bits >> 31) & np.int32(0x7FFFFFFF))


def _unsortable(key):
    return pltpu.bitcast(key ^ ((key >> 31) & np.int32(0x7FFFFFFF)), F32)


def _nt_dot(a, b):
    return lax.dot_general(a, b, (((1,), (1,)), ((), ())), preferred_element_type=F32)


def _mod_kernel(c_ref, w_ref, b_ref, out_ref):
    c = c_ref[...]
    c_act = c * (1.0 / (1.0 + jnp.exp(-c)))
    out_ref[0] = jnp.dot(c_act, w_ref[0], preferred_element_type=F32) + b_ref[0]


def _modulation(c, w_mod, b_mod):
    depth, d, n3 = w_mod.shape
    b = c.shape[0]
    bp = max(8, b)
    c_pad = jnp.zeros((bp, d), F32).at[:b].set(c)
    tn = 768
    out = pl.pallas_call(
        _mod_kernel,
        grid=(depth, n3 // tn),
        in_specs=[pl.BlockSpec((bp, d), lambda l, j: (0, 0)),
                  pl.BlockSpec((1, d, tn), lambda l, j: (l, 0, j)),
                  pl.BlockSpec((1, 1, tn), lambda l, j: (l, 0, j))],
        out_specs=pl.BlockSpec((1, bp, tn), lambda l, j: (l, 0, j)),
        out_shape=jax.ShapeDtypeStruct((depth, bp, n3), F32),
        compiler_params=pltpu.CompilerParams(dimension_semantics=("arbitrary", "arbitrary"),
                                             vmem_limit_bytes=VMEM_LIMIT),
        name="adaln_mod",
    )(c_pad, w_mod, b_mod.reshape(depth, 1, n3))
    return out[:, :b]


def _rope_kernel(pos_ref, inv_ref, sign_ref, out_ref):
    ang = pos_ref[...].astype(F32) * inv_ref[...]
    out_ref[:, :LANES] = jnp.cos(ang)
    out_ref[:, LANES:] = jnp.sin(ang) * sign_ref[...]


def _rope_tables(positions):
    m = positions.size
    half = HEAD_DIM // 2
    inv = ROPE_BASE ** (-jnp.arange(half, dtype=F32) / half)
    inv2 = jnp.concatenate([inv, inv]).reshape(1, LANES)
    sign = jnp.concatenate([-jnp.ones((half,), F32), jnp.ones((half,), F32)]).reshape(1, LANES)
    tm = min(2048, m)
    return pl.pallas_call(
        _rope_kernel,
        grid=(m // tm,),
        in_specs=[pl.BlockSpec((tm, 1), lambda i: (i, 0)),
                  pl.BlockSpec((1, LANES), lambda i: (0, 0)),
                  pl.BlockSpec((1, LANES), lambda i: (0, 0))],
        out_specs=pl.BlockSpec((tm, 2 * LANES), lambda i: (i, 0)),
        out_shape=jax.ShapeDtypeStruct((m, 2 * LANES), F32),
        compiler_params=pltpu.CompilerParams(dimension_semantics=("arbitrary",)),
        name="rope_tables",
    )(positions.reshape(m, 1), inv2, sign)


def _group_rmsnorm(seg, gain):
    ms = jnp.mean(seg * seg, axis=-1, keepdims=True)
    return seg * lax.rsqrt(ms + EPS) * gain


def _proj_kernel(x_ref, g_ref, scale_ref, shift_ref, w_ref, cs_ref, out_ref, kw_ref, h_ref):
    j = pl.program_id(1)
    subs = PROJ_TN // PROJ_SUB
    kiw_sub, kiw_lo = divmod(SEG_KIW % PROJ_TN, PROJ_SUB)

    def project(rows, h):
        for c in range(subs):
            sub = j * subs + c
            is_rot = (sub >= SEG_QB // PROJ_SUB) & (sub < SEG_VB // PROJ_SUB)
            rot_scale = jnp.where(sub >= SEG_KB // PROJ_SUB, HEAD_DIM ** -0.5, 1.0)
            acc = jnp.dot(h, w_ref[0, :, c * PROJ_SUB:(c + 1) * PROJ_SUB], preferred_element_type=F32)
            a_mul = jnp.where(is_rot, cs_ref[rows, :LANES] * rot_scale, 1.0)
            b_mul = jnp.where(is_rot, cs_ref[rows, LANES:] * rot_scale, 0.0)
            for a in range(PROJ_SUB // LANES):
                seg = acc[:, a * LANES:(a + 1) * LANES]
                col = c * PROJ_SUB + a * LANES
                out_ref[rows, col:col + LANES] = (
                    seg * a_mul + pltpu.roll(seg, HEAD_DIM // 2, axis=1) * b_mul).astype(BF16)
            if c == kiw_sub:
                kw_ref[rows, :] = acc[:, kiw_lo:kiw_lo + LANES]

    chunks = [slice(r * PROJ_ROWS, (r + 1) * PROJ_ROWS) for r in range(h_ref.shape[0] // PROJ_ROWS)]

    @pl.when(j == 0)
    def _():
        for rows in chunks:
            x = x_ref[rows, :]
            ms = jnp.mean(x * x, axis=-1, keepdims=True)
            y = x * lax.rsqrt(ms + EPS) * g_ref[...]
            h = (y * (1.0 + scale_ref[0]) + shift_ref[0]).astype(BF16)
            h_ref[rows, :] = h
            project(rows, h)

    @pl.when(j != 0)
    def _():
        for rows in chunks:
            project(rows, h_ref[rows, :])


def _pack_w_in(w):
    splits = [int(p) for p in np.cumsum(IN_SIZES)[:-1]]
    q_a, k_a, v_a, g_a, q_i, k_i, w_i, q_b, k_b, v_b, g_b = jnp.split(w, splits, axis=-1)
    pad = jnp.zeros(w.shape[:-1] + (N_PACK - SEG_KIW - IDX_DIM - IDX_HEADS,), w.dtype)
    parts = [q_a, g_a, q_i, q_b, k_b, v_b, g_b, k_a, v_a, k_i, w_i, pad]
    return jnp.concatenate([p.astype(BF16) for p in parts], axis=-1)


def _in_projection(x2, norm_gain, scale, shift, w_pack, layer, cs, seq):
    m, d = x2.shape
    tm = min(1024, seq)
    per_batch = seq // tm
    bsz = scale.shape[0]
    return pl.pallas_call(
        _proj_kernel,
        grid=(m // tm, N_PACK // PROJ_TN),
        in_specs=[pl.BlockSpec((tm, d), lambda i, j: (i, 0)),
                  pl.BlockSpec((1, d), lambda i, j: (0, 0)),
                  pl.BlockSpec((1, 1, d), lambda i, j: (i // per_batch, 0, 0)),
                  pl.BlockSpec((1, 1, d), lambda i, j: (i // per_batch, 0, 0)),
                  pl.BlockSpec((1, d, PROJ_TN), lambda i, j: (layer, 0, j)),
                  pl.BlockSpec((tm, 2 * LANES), lambda i, j: (i, 0))],
        out_specs=[pl.BlockSpec((tm, PROJ_TN), lambda i, j: (i, j)),
                   pl.BlockSpec((tm, LANES), lambda i, j: (i, 0))],
        out_shape=[jax.ShapeDtypeStruct((m, N_PACK), BF16),
                   jax.ShapeDtypeStruct((m, LANES), F32)],
        scratch_shapes=[pltpu.VMEM((tm, d), BF16)],
        compiler_params=pltpu.CompilerParams(dimension_semantics=("arbitrary", "arbitrary"),
                                             vmem_limit_bytes=VMEM_LIMIT),
        name="norm_in_proj",
    )(x2, norm_gain.reshape(1, d), scale.reshape(bsz, 1, d), shift.reshape(bsz, 1, d), w_pack, cs)


def _attn_kernel(pmin_ref, pmax_ref,
                 q_ref, g_ref, posr_ref, qi_ref, kwq_ref,
                 k_ref, v_ref, kiw_ref, posc_ref, tbl_ref, qg_ref, kg_ref,
                 out_ref,
                 keys_ref, thr_ref, vt_ref, klo_ref, khi_ref, kn_ref, acc_ref, m_ref, alpha_ref,
                 wt_ref, pen_ref, sc_ref, s_ref, p_ref, qt_ref, qit_ref, kmin_ref, kmax_ref,
                 *, tq, top_k, seq):
    b = pl.program_id(0)
    i = pl.program_id(1)
    n_q = seq // tq
    tk = tq
    cur = i % 2
    prv = 1 - cur
    i_att = i - 1
    scoring = i < n_q
    attending = i >= 1
    tiles_per_trip = 2 if n_q % 2 == 0 else 1

    def tile_rows(j):
        return pl.ds(pl.multiple_of(j * tk, tk), tk)

    @pl.when(i == 0)
    def _():
        def prep(c, carry):
            rows = tile_rows(c)
            vt_ref[:HEAD_DIM, rows] = v_ref[0, rows, :].astype(F32).T.astype(BF16)
            row = lax.broadcasted_iota(I32, (VT_ROWS - HEAD_DIM, tk), 0)
            vt_ref[HEAD_DIM:, rows] = jnp.where(row == 0, 1.0, 0.0).astype(BF16)
            kn_ref[rows, :] = _group_rmsnorm(k_ref[0, rows, :].astype(F32), kg_ref[...]).astype(BF16)
            kf = kiw_ref[0, rows, :].astype(F32)
            lane = lax.broadcasted_iota(I32, kf.shape, 1)
            klo_ref[rows, :] = jnp.where(lane < IDX_DIM, kf, 0.0).astype(BF16)
            khi_ref[rows, :] = jnp.where(lane >= IDX_DIM, pltpu.roll(kf, IDX_DIM, axis=1), 0.0).astype(BF16)
            return carry
        lax.fori_loop(0, seq // tk, prep, 0)

    def scoring_setup():
        wt_ref[...] = kwq_ref[0].T[IDX_DIM:IDX_DIM + IDX_HEADS, :]
        for p in range(IDX_HEADS // 2):
            qit_ref[p] = qi_ref[0, :, p * LANES:(p + 1) * LANES].astype(F32).T.astype(BF16)
        kmin_ref[...] = jnp.full(kmin_ref.shape, INT_MAX, I32)
        kmax_ref[...] = jnp.full(kmax_ref.shape, INT_MIN, I32)

    def attending_setup():
        q_gain = qg_ref[...] * (HEAD_DIM ** -0.5 * LOG2E)
        for h in range(N_HEADS_A):
            qn = _group_rmsnorm(q_ref[0, :, h * HEAD_DIM:(h + 1) * HEAD_DIM].astype(F32), q_gain)
            qt_ref[h] = qn.T.astype(BF16)
        m_ref[...] = jnp.full(m_ref.shape, NEG_BIG, F32)
        acc_ref[...] = jnp.zeros(acc_ref.shape, F32)

    def when_roles(score_part, attend_part, score_first):
        @pl.when(scoring & attending)
        def _():
            for part in ((score_part, attend_part) if score_first else (attend_part, score_part)):
                part()

        @pl.when(jnp.logical_not(attending))
        def _():
            score_part()

        @pl.when(jnp.logical_not(scoring))
        def _():
            attend_part()

    when_roles(scoring_setup, attending_setup, score_first=True)


    def score_slots(j, slots):
        rows = tile_rows(j)
        term = None
        for p in slots:
            d_lo = jnp.dot(klo_ref[rows, :], qit_ref[p], preferred_element_type=F32)
            d_hi = jnp.dot(khi_ref[rows, :], qit_ref[p], preferred_element_type=F32)
            t = (wt_ref[2 * p:2 * p + 1, :] * jnp.maximum(d_lo, 0.0)
                 + wt_ref[2 * p + 1:2 * p + 2, :] * jnp.maximum(d_hi, 0.0))
            term = t if term is None else term + t
        sc_ref[...] = term if slots[0] == 0 else sc_ref[...] + term

    def score_finish(j, diagonal):
        key = _sortable(sc_ref[...])
        low = key
        if diagonal:
            causal = (lax.broadcasted_iota(I32, (tk, tq), 0) <= lax.broadcasted_iota(I32, (tk, tq), 1))
            key = jnp.where(causal, key, INT_MIN)
            low = jnp.where(causal, low, INT_MAX)
        keys_ref[cur, tile_rows(j), :] = key
        kmin_ref[...] = jnp.minimum(kmin_ref[...], jnp.min(low.reshape(tk // 8, 8, tq), axis=0))
        kmax_ref[...] = jnp.maximum(kmax_ref[...], jnp.max(key.reshape(tk // 8, 8, tq), axis=0))

    def attend_tile(j, with_bias, score=False):
        rows = tile_rows(j)
        kt = kn_ref[rows, :]
        pen_ref[...] = jnp.where(keys_ref[prv, rows, :] >= thr_ref[prv], 0.0, NEG_BIG)
        if with_bias:
            rel = jnp.clip(posr_ref[0] - posc_ref[0, rows, :], 0, LANES - 1)
        for h in range(N_HEADS_A):
            s = jnp.dot(kt, qt_ref[h], preferred_element_type=F32) + pen_ref[...]
            if with_bias:
                tb = jnp.broadcast_to(tbl_ref[h:h + 1, :], (tk, LANES))
                bias = [jnp.take_along_axis(tb, rel[:, c * LANES:(c + 1) * LANES], axis=1)
                        for c in range(tq // LANES)]
                s = s + jnp.concatenate(bias, axis=1)
            s_ref[h] = s
            m_old = m_ref[h]
            m_new = jnp.maximum(m_old, jnp.max(s_ref[h], axis=0, keepdims=True))
            alpha_ref[h] = jnp.exp2(m_old - m_new)
            m_ref[h] = m_new
        for h in range(N_HEADS_A):
            p_ref[h] = jnp.exp2(s_ref[h] - m_ref[h]).astype(BF16)
            if score:
                score_slots(j, (h,))
        for h in range(N_HEADS_A):
            acc_ref[h] = alpha_ref[h] * acc_ref[h] + jnp.dot(vt_ref[:, rows], p_ref[h],
                                                             preferred_element_type=F32)
        if score:
            score_finish(j, diagonal=False)

    def is_near(j):
        return pmin_ref[b, i_att] - pmax_ref[b, j] < FAR_REL

    def fused_tile(j, carry):
        near = is_near(j)

        @pl.when(near)
        def _():
            attend_tile(j, True, score=True)

        @pl.when(jnp.logical_not(near))
        def _():
            attend_tile(j, False, score=True)
        return carry

    def attend_only_tile(j, carry):
        near = is_near(j)

        @pl.when(near)
        def _():
            attend_tile(j, True)

        @pl.when(jnp.logical_not(near))
        def _():
            attend_tile(j, False)
        return carry

    @pl.when(scoring & attending)
    def _():
        lax.fori_loop(0, i, fused_tile, 0)

    @pl.when(jnp.logical_not(scoring))
    def _():
        lax.fori_loop(0, i, attend_only_tile, 0)

    def diagonal_score():
        for p in range(IDX_HEADS // 2):
            score_slots(i, (p,))
        score_finish(i, diagonal=True)

    def write_output():
        for h in range(N_HEADS_A):
            sl = slice(h * HEAD_DIM, (h + 1) * HEAD_DIM)
            acc = acc_ref[h]
            o = (acc[:HEAD_DIM] / acc[HEAD_DIM:HEAD_DIM + 1]).T
            g = g_ref[0, :, sl].astype(F32)
            out_ref[0, :, sl] = (o * (g / (1.0 + jnp.exp(-g)))).astype(BF16)

    when_roles(diagonal_score, write_output, score_first=False)

    if tiles_per_trip == 2:
        @pl.when(scoring & (i % 2 == 0))
        def _():
            keys_ref[cur, tile_rows(i + 1), :] = jnp.full((tk, tq), INT_MIN, I32)

    @pl.when(scoring)
    def _():
        n_kt = i + 1

        def count_where(pred):
            def body(jj, c):
                for t in range(tiles_per_trip):
                    j = jj * tiles_per_trip + t
                    hit = pred(j, keys_ref[cur, tile_rows(j), :]).astype(I32)
                    c = c + jnp.sum(hit.reshape(COUNT_CHAINS, tk // (8 * COUNT_CHAINS), 8, tq), axis=1)
                return c
            c8 = lax.fori_loop(0, (n_kt + tiles_per_trip - 1) // tiles_per_trip, body,
                               jnp.zeros((COUNT_CHAINS, 8, tq), I32))
            return jnp.sum(jnp.sum(c8, axis=0), axis=0, keepdims=True)

        key_min = jnp.min(kmin_ref[...], axis=0, keepdims=True)
        key_max = jnp.max(kmax_ref[...], axis=0, keepdims=True)

        n_causal = i * tq + lax.broadcasted_iota(I32, (1, tq), 1) + 1
        short = n_causal <= top_k
        lo0 = jnp.where(short, INT_MIN + 1, key_min)
        hi0 = jnp.where(short, INT_MIN + 2, jnp.where(key_max == INT_MAX, key_max, key_max + 1))
        cnt0 = jnp.where(short, top_k, n_causal)

        def active(lo, hi, cnt):
            return (cnt != top_k) & (hi > lo + 1)

        def any_lane(mask):
            return jnp.max(jnp.where(mask, 1.0, 0.0)) > 0.0

        def search_cond(c):
            it, lo, hi, cnt, _ = c
            return (it < SEARCH_MAX_STEPS) & any_lane(active(lo, hi, cnt))

        def search_body(c):
            it, lo, hi, cnt, cnt_hi = c
            for u in range(SEARCH_UNROLL):
                mid_key = (lo & hi) + ((lo ^ hi) >> 1)
                v_lo, v_hi = _unsortable(lo), _unsortable(hi)
                above = (cnt - top_k).astype(F32) + 0.5
                interp = above / jnp.maximum(cnt - cnt_hi, 1).astype(F32)
                frac = 0.5 + SEARCH_INTERP_WEIGHT * (interp - 0.5)
                mid_val = _sortable(v_lo + (v_hi - v_lo) * frac)
                piv = jnp.where(it + u < SEARCH_VALUE_STEPS, mid_val, mid_key)
                piv = jnp.minimum(jnp.maximum(piv, lo + 1), hi - 1)
                c_piv = count_where(lambda j, kt: kt >= piv)
                act = active(lo, hi, cnt)
                up = act & (c_piv >= top_k)
                down = act & (c_piv < top_k)
                lo, cnt = jnp.where(up, piv, lo), jnp.where(up, c_piv, cnt)
                hi, cnt_hi = jnp.where(down, piv, hi), jnp.where(down, c_piv, cnt_hi)
            return it + SEARCH_UNROLL, lo, hi, cnt, cnt_hi
        _, thr, _, cnt_thr, _ = lax.while_loop(search_cond, search_body,
                                               (jnp.int32(0), lo0, hi0, cnt0, jnp.zeros((1, tq), I32)))
        thr_ref[cur] = thr

        tied = cnt_thr > top_k

        @pl.when(any_lane(tied))
        def _():
            need = top_k - count_where(lambda j, kt: kt > thr)
            s_of = lambda j: j * tk + lax.broadcasted_iota(I32, (tk, tq), 0)

            def cut_step(n, c):
                c_lo, c_hi = c
                mid = (c_lo + c_hi) >> 1
                enough = count_where(lambda j, kt: (kt == thr) & (s_of(j) < mid)) >= need
                return jnp.where(enough, c_lo, mid), jnp.where(enough, mid, c_hi)
            _, cut = lax.fori_loop(0, max(1, (seq - 1).bit_length()), cut_step,
                                   (jnp.zeros((1, tq), I32), jnp.full((1, tq), seq, I32)))

            def demote(j, carry):
                kt = keys_ref[cur, tile_rows(j), :]
                drop = tied & (kt == thr) & (s_of(j) >= cut)
                keys_ref[cur, tile_rows(j), :] = jnp.where(drop, INT_MIN, kt)
                return carry
            lax.fori_loop(0, n_kt, demote, 0)


def _bias_table(rel_bias):
    r = jnp.arange(LANES, dtype=I32)
    relf = jnp.maximum(r, 1).astype(F32)
    large = MAX_EXACT + (jnp.log(relf / MAX_EXACT) / math.log(MAX_DISTANCE / MAX_EXACT)
                         * (N_BUCKETS - MAX_EXACT)).astype(I32)
    bucket = jnp.where(r < MAX_EXACT, r, jnp.minimum(large, N_BUCKETS - 1))
    return ((rel_bias[bucket] - rel_bias[N_BUCKETS - 1][None, :]) * LOG2E).T.astype(F32)


def _sparse_attention(proj3, kw3, positions, tbl, pmin, pmax, q_gain, k_gain, tq):
    bsz, seq, _ = proj3.shape
    top_k = min(TOPK_MAX, seq // 4)
    n_q = seq // tq
    att = lambda b, i, *_: (b, jnp.maximum(i - 1, 0))
    sco = lambda b, i, *_: (b, jnp.minimum(i, n_q - 1))
    wide = lambda seg, tile: pl.BlockSpec((1, tq, 1024), lambda *a: (*tile(*a), seg // 1024))
    full = lambda seg: pl.BlockSpec((1, seq, LANES), lambda b, i, *_: (b, 0, seg // LANES))
    fixed = lambda shape: pl.BlockSpec(shape, lambda b, i, *_: (0,) * len(shape))
    grid_spec = pltpu.PrefetchScalarGridSpec(
        num_scalar_prefetch=2,
        grid=(bsz, n_q + 1),
        in_specs=[wide(SEG_QA, att), wide(SEG_GA, att),
                  pl.BlockSpec((1, 1, tq), lambda b, i, *_: (b, 0, jnp.maximum(i - 1, 0))),
                  wide(SEG_QI, sco),
                  pl.BlockSpec((1, tq, LANES), lambda *a: (*sco(*a), 0)),
                  full(SEG_KA), full(SEG_VA), full(SEG_KIW),
                  pl.BlockSpec((1, seq, 1), lambda b, i, *_: (b, 0, 0)),
                  fixed((N_HEADS_A, LANES)), fixed((1, LANES)), fixed((1, LANES))],
        out_specs=pl.BlockSpec((1, tq, D_A), lambda *a: (*att(*a), 0)),
        scratch_shapes=[pltpu.VMEM((2, seq, tq), I32),
                        pltpu.VMEM((2, 1, tq), I32),
                        pltpu.VMEM((VT_ROWS, seq), BF16),
                        pltpu.VMEM((seq, LANES), BF16),
                        pltpu.VMEM((seq, LANES), BF16),
                        pltpu.VMEM((seq, HEAD_DIM), BF16),
                        pltpu.VMEM((N_HEADS_A, VT_ROWS, tq), F32),
                        pltpu.VMEM((N_HEADS_A, 1, tq), F32),
                        pltpu.VMEM((N_HEADS_A, 1, tq), F32),
                        pltpu.VMEM((IDX_HEADS, tq), F32),
                        pltpu.VMEM((tq, tq), F32),
                        pltpu.VMEM((tq, tq), F32),
                        pltpu.VMEM((N_HEADS_A, tq, tq), F32),
                        pltpu.VMEM((N_HEADS_A, tq, tq), BF16),
                        pltpu.VMEM((N_HEADS_A, HEAD_DIM, tq), BF16),
                        pltpu.VMEM((IDX_HEADS // 2, LANES, tq), BF16),
                        pltpu.VMEM((8, tq), I32),
                        pltpu.VMEM((8, tq), I32)])
    return pl.pallas_call(
        functools.partial(_attn_kernel, tq=tq, top_k=top_k, seq=seq),
        grid_spec=grid_spec,
        out_shape=jax.ShapeDtypeStruct((bsz, seq, D_A), BF16),
        compiler_params=pltpu.CompilerParams(dimension_semantics=("arbitrary", "arbitrary"),
                                             vmem_limit_bytes=VMEM_LIMIT),
        name="sparse_attention",
    )(pmin, pmax, proj3, proj3, positions.reshape(bsz, 1, seq), proj3, kw3, proj3, proj3, proj3,
      positions.reshape(bsz, seq, 1), tbl, q_gain.reshape(1, LANES), k_gain.reshape(1, LANES))


def _ret_kernel(gch_ref, q_ref, k_ref, v_ref, g_ref, dmask_ref, zeta_ref, xi_ref, gain_ref,
                out_ref, state_ref, intra_ref, u_ref, *, tc):
    c = pl.program_id(1)
    chunks = [slice(cc * RET_CHUNK, (cc + 1) * RET_CHUNK) for cc in range(tc // RET_CHUNK)]
    heads = [slice(h * HEAD_DIM, (h + 1) * HEAD_DIM) for h in range(N_HEADS_B)]

    @pl.when(c == 0)
    def _():
        state_ref[...] = jnp.zeros(state_ref.shape, F32)

    for h, sl in enumerate(heads):
        for cc, rows in enumerate(chunks):
            q = q_ref[0, rows, sl]
            k = k_ref[0, rows, sl]
            v = v_ref[0, rows, sl]
            s_in = _nt_dot(q, k) * dmask_ref[h]
            intra_ref[h, cc] = jnp.dot(s_in.astype(BF16), v, preferred_element_type=F32)
            kz_t = (k.astype(F32) * zeta_ref[h]).T.astype(BF16)
            u_ref[h, cc] = jnp.dot(kz_t, v, preferred_element_type=F32)

    for h, sl in enumerate(heads):
        state = state_ref[h]
        for cc, rows in enumerate(chunks):
            cross = jnp.dot(q_ref[0, rows, sl], state.astype(BF16), preferred_element_type=F32) * xi_ref[h]
            o = intra_ref[h, cc] + cross
            state = gch_ref[h] * state + u_ref[h, cc]
            mu = jnp.mean(o, axis=-1, keepdims=True)
            var = jnp.mean(jnp.square(o - mu), axis=-1, keepdims=True)
            y = (o - mu) * lax.rsqrt(var + EPS) * gain_ref[:, sl]
            g = g_ref[0, rows, sl].astype(F32)
            out_ref[0, rows, sl] = (y * (g / (1.0 + jnp.exp(-g)))).astype(BF16)
        state_ref[h] = state


def _retention_tables():
    c = RET_CHUNK
    log_g = jnp.log1p(-jnp.exp2(-5.0 - jnp.arange(N_HEADS_B, dtype=F32)))
    i = jnp.arange(c, dtype=F32)
    diff = i[:, None] - i[None, :]
    dmask = jnp.where(diff >= 0, jnp.exp(log_g[:, None, None] * jnp.maximum(diff, 0.0)), 0.0)
    zeta = jnp.exp(log_g[:, None] * (c - 1.0 - i)[None, :])
    xi = jnp.exp(log_g[:, None] * (i + 1.0)[None, :])
    g_chunk = jnp.exp(log_g * c)
    widen = lambda a: jnp.broadcast_to(a[:, :, None], (N_HEADS_B, c, HEAD_DIM))
    return g_chunk, dmask, widen(zeta), widen(xi)


def _retention(proj3, ret_gain, tables):
    bsz, seq, _ = proj3.shape
    tc = min(512, seq)
    g_chunk, dmask, zeta, xi = tables
    wide = lambda seg: pl.BlockSpec((1, tc, 1024), lambda b, c: (b, c, seg // 1024))
    table = pl.BlockSpec((N_HEADS_B, RET_CHUNK, HEAD_DIM), lambda b, c: (0, 0, 0))
    return pl.pallas_call(
        functools.partial(_ret_kernel, tc=tc),
        grid=(bsz, seq // tc),
        in_specs=[pl.BlockSpec(memory_space=pltpu.SMEM),
                  wide(SEG_QB), wide(SEG_KB), wide(SEG_VB), wide(SEG_GB),
                  table, table, table,
                  pl.BlockSpec((1, D_B), lambda b, c: (0, 0))],
        out_specs=pl.BlockSpec((1, tc, D_B), lambda b, c: (b, c, 0)),
        out_shape=jax.ShapeDtypeStruct((bsz, seq, D_B), BF16),
        scratch_shapes=[pltpu.VMEM((N_HEADS_B, HEAD_DIM, HEAD_DIM), F32),
                        pltpu.VMEM((N_HEADS_B, tc // RET_CHUNK, RET_CHUNK, HEAD_DIM), F32),
                        pltpu.VMEM((N_HEADS_B, tc // RET_CHUNK, HEAD_DIM, HEAD_DIM), F32)],
        compiler_params=pltpu.CompilerParams(dimension_semantics=("arbitrary", "arbitrary"),
                                             vmem_limit_bytes=VMEM_LIMIT),
        name="retention",
    )(g_chunk, proj3, proj3, proj3, proj3, dmask, zeta, xi, ret_gain.reshape(1, D_B))


def _out_kernel(ya_ref, yb_ref, wa_ref, wb_ref, x_ref, gate_ref, out_ref):
    y = jnp.dot(ya_ref[...], wa_ref[0], preferred_element_type=F32)
    y = y + jnp.dot(yb_ref[...], wb_ref[0], preferred_element_type=F32)
    out_ref[...] = x_ref[...] + gate_ref[0] * y


def _out_projection(ya2, yb2, w_out, layer, x2, gate, seq):
    m, d = x2.shape
    tm = min(1024, seq)
    tn = 1024
    per_batch = seq // tm
    bsz = gate.shape[0]
    return pl.pallas_call(
        _out_kernel,
        grid=(m // tm, d // tn),
        in_specs=[pl.BlockSpec((tm, D_A), lambda i, j: (i, 0)),
                  pl.BlockSpec((tm, D_B), lambda i, j: (i, 0)),
                  pl.BlockSpec((1, D_A, tn), lambda i, j: (layer, 0, j)),
                  pl.BlockSpec((1, D_B, tn), lambda i, j: (layer, 1, j)),
                  pl.BlockSpec((tm, tn), lambda i, j: (i, j)),
                  pl.BlockSpec((1, 1, tn), lambda i, j: (i // per_batch, 0, j))],
        out_specs=pl.BlockSpec((tm, tn), lambda i, j: (i, j)),
        out_shape=jax.ShapeDtypeStruct((m, d), F32),
        compiler_params=pltpu.CompilerParams(dimension_semantics=("arbitrary", "arbitrary"),
                                             vmem_limit_bytes=VMEM_LIMIT),
        name="out_proj_residual",
    )(ya2, yb2, w_out, w_out, x2, gate.reshape(bsz, 1, d))


def kernel(x, c, positions, rel_bias, norm_gain, w_mod, b_mod, w_in, q_norm_gain, k_norm_gain,
           ret_norm_gain, w_out):
    bsz, seq, d = x.shape
    depth = w_in.shape[0]
    m = bsz * seq
    tq = min(256, seq)
    assert d == D_A + D_B and seq % tq == 0 and seq % RET_CHUNK == 0

    mod = _modulation(c, w_mod, b_mod)
    cs = _rope_tables(positions)
    tbl = _bias_table(rel_bias)
    pos_tiles = positions.reshape(bsz, seq // tq, tq)
    pmin = jnp.min(pos_tiles, axis=-1).astype(I32)
    pmax = jnp.max(pos_tiles, axis=-1).astype(I32)
    tables = _retention_tables()
    w_in_packed = _pack_w_in(w_in)
    w_out_bf16 = w_out.astype(BF16)

    x2 = x.reshape(m, d)
    for l in range(depth):
        shift, scale, gate = mod[l, :, :d], mod[l, :, d:2 * d], mod[l, :, 2 * d:]
        proj, kw = _in_projection(x2, norm_gain[l], scale, shift, w_in_packed, l, cs, seq)
        proj3 = proj.reshape(bsz, seq, N_PACK)
        y_a = _sparse_attention(proj3, kw.reshape(bsz, seq, LANES), positions, tbl, pmin, pmax,
                                q_norm_gain[l], k_norm_gain[l], tq)
        y_b = _retention(proj3, ret_norm_gain[l], tables)
        x2 = _out_projection(y_a.reshape(m, D_A), y_b.reshape(m, D_B), w_out_bf16, l, x2, gate, seq)
    return x2.reshape(bsz, seq, d)
```

```python
import functools
import math

import numpy as np
import jax
import jax.numpy as jnp
from jax import lax
from jax.experimental import pallas as pl
from jax.experimental.pallas import tpu as pltpu

F32 = jnp.float32
BF16 = jnp.bfloat16
I32 = jnp.int32

HEAD_DIM = 128
N_HEADS_A = 8
N_HEADS_B = 8
D_A = N_HEADS_A * HEAD_DIM
D_B = N_HEADS_B * HEAD_DIM
IDX_HEADS = 16
IDX_DIM = 64
TOPK_MAX = 256
RET_CHUNK = 128
N_BUCKETS = 32
MAX_EXACT = 16
MAX_DISTANCE = 128
ROPE_BASE = 10000.0
EPS = 1e-6
IN_SIZES = (D_A, HEAD_DIM, HEAD_DIM, D_A, IDX_HEADS * IDX_DIM, IDX_DIM, IDX_HEADS, D_B, D_B, D_B, D_B)

LANES = 128
VMEM_LIMIT = 56 * 1024 * 1024

SEG_QA, SEG_GA, SEG_QI, SEG_QB, SEG_KB, SEG_VB, SEG_GB = (n * 1024 for n in range(7))
SEG_KA = 7 * 1024
SEG_VA = SEG_KA + LANES
SEG_KIW = SEG_VA + LANES
N_PACK = 7 * 1024 + 512
PROJ_TN = 1536
PROJ_SUB = 512
PROJ_ROWS = 256

INT_MIN = np.int32(-2 ** 31)
INT_MAX = np.int32(2 ** 31 - 1)
LOG2E = math.log2(math.e)
NEG_BIG = -1e30
VT_ROWS = HEAD_DIM + 16
FAR_REL = 113
COUNT_CHAINS = 4
SEARCH_UNROLL = 3
SEARCH_VALUE_STEPS = 12
SEARCH_INTERP_WEIGHT = 0.7
SEARCH_MAX_STEPS = SEARCH_VALUE_STEPS + 33


def _sortable(x):
    bits = pltpu.bitcast(x, I32)
    return bits ^ ((bits >> 31) & np.int32(0x7FFFFFFF))


def _unsortable(key):
    return pltpu.bitcast(key ^ ((key >> 31) & np.int32(0x7FFFFFFF)), F32)


def _nt_dot(a, b):
    return lax.dot_general(a, b, (((1,), (1,)), ((), ())), preferred_element_type=F32)


def _mod_kernel(c_ref, w_ref, b_ref, out_ref):
    c = c_ref[...]
    c_act = c * (1.0 / (1.0 + jnp.exp(-c)))
    out_ref[0] = jnp.dot(c_act, w_ref[0], preferred_element_type=F32) + b_ref[0]


def _modulation(c, w_mod, b_mod):
    depth, d, n3 = w_mod.shape
    b = c.shape[0]
    bp = max(8, b)
    c_pad = jnp.zeros((bp, d), F32).at[:b].set(c)
    tn = 768
    out = pl.pallas_call(
        _mod_kernel,
        grid=(depth, n3 // tn),
        in_specs=[pl.BlockSpec((bp, d), lambda l, j: (0, 0)),
                  pl.BlockSpec((1, d, tn), lambda l, j: (l, 0, j)),
                  pl.BlockSpec((1, 1, tn), lambda l, j: (l, 0, j))],
        out_specs=pl.BlockSpec((1, bp, tn), lambda l, j: (l, 0, j)),
        out_shape=jax.ShapeDtypeStruct((depth, bp, n3), F32),
        compiler_params=pltpu.CompilerParams(dimension_semantics=("arbitrary", "arbitrary"),
                                             vmem_limit_bytes=VMEM_LIMIT),
        name="adaln_mod",
    )(c_pad, w_mod, b_mod.reshape(depth, 1, n3))
    return out[:, :b]


def _rope_kernel(pos_ref, inv_ref, sign_ref, out_ref):
    ang = pos_ref[...].astype(F32) * inv_ref[...]
    out_ref[:, :LANES] = jnp.cos(ang)
    out_ref[:, LANES:] = jnp.sin(ang) * sign_ref[...]


def _rope_tables(positions):
    m = positions.size
    half = HEAD_DIM // 2
    inv = ROPE_BASE ** (-jnp.arange(half, dtype=F32) / half)
    inv2 = jnp.concatenate([inv, inv]).reshape(1, LANES)
    sign = jnp.concatenate([-jnp.ones((half,), F32), jnp.ones((half,), F32)]).reshape(1, LANES)
    tm = min(2048, m)
    return pl.pallas_call(
        _rope_kernel,
        grid=(m // tm,),
        in_specs=[pl.BlockSpec((tm, 1), lambda i: (i, 0)),
                  pl.BlockSpec((1, LANES), lambda i: (0, 0)),
                  pl.BlockSpec((1, LANES), lambda i: (0, 0))],
        out_specs=pl.BlockSpec((tm, 2 * LANES), lambda i: (i, 0)),
        out_shape=jax.ShapeDtypeStruct((m, 2 * LANES), F32),
        compiler_params=pltpu.CompilerParams(dimension_semantics=("arbitrary",)),
        name="rope_tables",
    )(positions.reshape(m, 1), inv2, sign)


def _group_rmsnorm(seg, gain):
    ms = jnp.mean(seg * seg, axis=-1, keepdims=True)
    return seg * lax.rsqrt(ms + EPS) * gain


def _proj_kernel(x_ref, g_ref, scale_ref, shift_ref, w_ref, cs_ref, out_ref, kw_ref, h_ref):
    j = pl.program_id(1)
    subs = PROJ_TN // PROJ_SUB
    kiw_sub, kiw_lo = divmod(SEG_KIW % PROJ_TN, PROJ_SUB)

    def project(rows, h):
        for c in range(subs):
            sub = j * subs + c
            is_rot = (sub >= SEG_QB // PROJ_SUB) & (sub < SEG_VB // PROJ_SUB)
            rot_scale = jnp.where(sub >= SEG_KB // PROJ_SUB, HEAD_DIM ** -0.5, 1.0)
            acc = jnp.dot(h, w_ref[0, :, c * PROJ_SUB:(c + 1) * PROJ_SUB], preferred_element_type=F32)
            a_mul = jnp.where(is_rot, cs_ref[rows, :LANES] * rot_scale, 1.0)
            b_mul = jnp.where(is_rot, cs_ref[rows, LANES:] * rot_scale, 0.0)
            for a in range(PROJ_SUB // LANES):
                seg = acc[:, a * LANES:(a + 1) * LANES]
                col = c * PROJ_SUB + a * LANES
                out_ref[rows, col:col + LANES] = (
                    seg * a_mul + pltpu.roll(seg, HEAD_DIM // 2, axis=1) * b_mul).astype(BF16)
            if c == kiw_sub:
                kw_ref[rows, :] = acc[:, kiw_lo:kiw_lo + LANES]

    chunks = [slice(r * PROJ_ROWS, (r + 1) * PROJ_ROWS) for r in range(h_ref.shape[0] // PROJ_ROWS)]

    @pl.when(j == 0)
    def _():
        for rows in chunks:
            x = x_ref[rows, :]
            ms = jnp.mean(x * x, axis=-1, keepdims=True)
            y = x * lax.rsqrt(ms + EPS) * g_ref[...]
            h = (y * (1.0 + scale_ref[0]) + shift_ref[0]).astype(BF16)
            h_ref[rows, :] = h
            project(rows, h)

    @pl.when(j != 0)
    def _():
        for rows in chunks:
            project(rows, h_ref[rows, :])


def _pack_w_in(w):
    splits = [int(p) for p in np.cumsum(IN_SIZES)[:-1]]
    q_a, k_a, v_a, g_a, q_i, k_i, w_i, q_b, k_b, v_b, g_b = jnp.split(w, splits, axis=-1)
    pad = jnp.zeros(w.shape[:-1] + (N_PACK - SEG_KIW - IDX_DIM - IDX_HEADS,), w.dtype)
    parts = [q_a, g_a, q_i, q_b, k_b, v_b, g_b, k_a, v_a, k_i, w_i, pad]
    return jnp.concatenate([p.astype(BF16) for p in parts], axis=-1)


def _in_projection(x2, norm_gain, scale, shift, w_pack, layer, cs, seq):
    m, d = x2.shape
    tm = min(1024, seq)
    per_batch = seq // tm
    bsz = scale.shape[0]
    return pl.pallas_call(
        _proj_kernel,
        grid=(m // tm, N_PACK // PROJ_TN),
        in_specs=[pl.BlockSpec((tm, d), lambda i, j: (i, 0)),
                  pl.BlockSpec((1, d), lambda i, j: (0, 0)),
                  pl.BlockSpec((1, 1, d), lambda i, j: (i // per_batch, 0, 0)),
                  pl.BlockSpec((1, 1, d), lambda i, j: (i // per_batch, 0, 0)),
                  pl.BlockSpec((1, d, PROJ_TN), lambda i, j: (layer, 0, j)),
                  pl.BlockSpec((tm, 2 * LANES), lambda i, j: (i, 0))],
        out_specs=[pl.BlockSpec((tm, PROJ_TN), lambda i, j: (i, j)),
                   pl.BlockSpec((tm, LANES), lambda i, j: (i, 0))],
        out_shape=[jax.ShapeDtypeStruct((m, N_PACK), BF16),
                   jax.ShapeDtypeStruct((m, LANES), F32)],
        scratch_shapes=[pltpu.VMEM((tm, d), BF16)],
        compiler_params=pltpu.CompilerParams(dimension_semantics=("arbitrary", "arbitrary"),
                                             vmem_limit_bytes=VMEM_LIMIT),
        name="norm_in_proj",
    )(x2, norm_gain.reshape(1, d), scale.reshape(bsz, 1, d), shift.reshape(bsz, 1, d), w_pack, cs)


def _attn_kernel(pmin_ref, pmax_ref,
                 q_ref, g_ref, posr_ref, qi_ref, kwq_ref,
                 k_ref, v_ref, kiw_ref, posc_ref, tbl_ref, qg_ref, kg_ref,
                 out_ref,
                 keys_ref, thr_ref, vt_ref, klo_ref, khi_ref, kn_ref, acc_ref, m_ref, alpha_ref,
                 wt_ref, pen_ref, sc_ref, s_ref, p_ref, qt_ref, qit_ref, kmin_ref, kmax_ref,
                 *, tq, top_k, seq):
    b = pl.program_id(0)
    i = pl.program_id(1)
    n_q = seq // tq
    tk = tq
    cur = i % 2
    prv = 1 - cur
    i_att = i - 1
    scoring = i < n_q
    attending = i >= 1
    tiles_per_trip = 2 if n_q % 2 == 0 else 1

    def tile_rows(j):
        return pl.ds(pl.multiple_of(j * tk, tk), tk)

    @pl.when(i == 0)
    def _():
        def prep(c, carry):
            rows = tile_rows(c)
            vt_ref[:HEAD_DIM, rows] = v_ref[0, rows, :].astype(F32).T.astype(BF16)
            row = lax.broadcasted_iota(I32, (VT_ROWS - HEAD_DIM, tk), 0)
            vt_ref[HEAD_DIM:, rows] = jnp.where(row == 0, 1.0, 0.0).astype(BF16)
            kn_ref[rows, :] = _group_rmsnorm(k_ref[0, rows, :].astype(F32), kg_ref[...]).astype(BF16)
            kf = kiw_ref[0, rows, :].astype(F32)
            lane = lax.broadcasted_iota(I32, kf.shape, 1)
            klo_ref[rows, :] = jnp.where(lane < IDX_DIM, kf, 0.0).astype(BF16)
            khi_ref[rows, :] = jnp.where(lane >= IDX_DIM, pltpu.roll(kf, IDX_DIM, axis=1), 0.0).astype(BF16)
            return carry
        lax.fori_loop(0, seq // tk, prep, 0)

    def scoring_setup():
        wt_ref[...] = kwq_ref[0].T[IDX_DIM:IDX_DIM + IDX_HEADS, :]
        for p in range(IDX_HEADS // 2):
            qit_ref[p] = qi_ref[0, :, p * LANES:(p + 1) * LANES].astype(F32).T.astype(BF16)
        kmin_ref[...] = jnp.full(kmin_ref.shape, INT_MAX, I32)
        kmax_ref[...] = jnp.full(kmax_ref.shape, INT_MIN, I32)

    def attending_setup():
        q_gain = qg_ref[...] * (HEAD_DIM ** -0.5 * LOG2E)
        for h in range(N_HEADS_A):
            qn = _group_rmsnorm(q_ref[0, :, h * HEAD_DIM:(h + 1) * HEAD_DIM].astype(F32), q_gain)
            qt_ref[h] = qn.T.astype(BF16)
        m_ref[...] = jnp.full(m_ref.shape, NEG_BIG, F32)
        acc_ref[...] = jnp.zeros(acc_ref.shape, F32)

    def when_roles(score_part, attend_part, score_first):
        @pl.when(scoring & attending)
        def _():
            for part in ((score_part, attend_part) if score_first else (attend_part, score_part)):
                part()

        @pl.when(jnp.logical_not(attending))
        def _():
            score_part()

        @pl.when(jnp.logical_not(scoring))
        def _():
            attend_part()

    when_roles(scoring_setup, attending_setup, score_first=True)


    def score_slots(j, slots):
        rows = tile_rows(j)
        term = None
        for p in slots:
            d_lo = jnp.dot(klo_ref[rows, :], qit_ref[p], preferred_element_type=F32)
            d_hi = jnp.dot(khi_ref[rows, :], qit_ref[p], preferred_element_type=F32)
            t = (wt_ref[2 * p:2 * p + 1, :] * jnp.maximum(d_lo, 0.0)
                 + wt_ref[2 * p + 1:2 * p + 2, :] * jnp.maximum(d_hi, 0.0))
            term = t if term is None else term + t
        sc_ref[...] = term if slots[0] == 0 else sc_ref[...] + term

    def score_finish(j, diagonal):
        key = _sortable(sc_ref[...])
        low = key
        if diagonal:
            causal = (lax.broadcasted_iota(I32, (tk, tq), 0) <= lax.broadcasted_iota(I32, (tk, tq), 1))
            key = jnp.where(causal, key, INT_MIN)
            low = jnp.where(causal, low, INT_MAX)
        keys_ref[cur, tile_rows(j), :] = key
        kmin_ref[...] = jnp.minimum(kmin_ref[...], jnp.min(low.reshape(tk // 8, 8, tq), axis=0))
        kmax_ref[...] = jnp.maximum(kmax_ref[...], jnp.max(key.reshape(tk // 8, 8, tq), axis=0))

    def attend_tile(j, with_bias, score=False):
        rows = tile_rows(j)
        kt = kn_ref[rows, :]
        pen_ref[...] = jnp.where(keys_ref[prv, rows, :] >= thr_ref[prv], 0.0, NEG_BIG)
        if with_bias:
            rel = jnp.clip(posr_ref[0] - posc_ref[0, rows, :], 0, LANES - 1)
        for h in range(N_HEADS_A):
            s = jnp.dot(kt, qt_ref[h], preferred_element_type=F32) + pen_ref[...]
            if with_bias:
                tb = jnp.broadcast_to(tbl_ref[h:h + 1, :], (tk, LANES))
                bias = [jnp.take_along_axis(tb, rel[:, c * LANES:(c + 1) * LANES], axis=1)
                        for c in range(tq // LANES)]
                s = s + jnp.concatenate(bias, axis=1)
            s_ref[h] = s
            m_old = m_ref[h]
            m_new = jnp.maximum(m_old, jnp.max(s_ref[h], axis=0, keepdims=True))
            alpha_ref[h] = jnp.exp2(m_old - m_new)
            m_ref[h] = m_new
        for h in range(N_HEADS_A):
            p_ref[h] = jnp.exp2(s_ref[h] - m_ref[h]).astype(BF16)
            if score:
                score_slots(j, (h,))
        for h in range(N_HEADS_A):
            acc_ref[h] = alpha_ref[h] * acc_ref[h] + jnp.dot(vt_ref[:, rows], p_ref[h],
                                                             preferred_element_type=F32)
        if score:
            score_finish(j, diagonal=False)

    def is_near(j):
        return pmin_ref[b, i_att] - pmax_ref[b, j] < FAR_REL

    def fused_tile(j, carry):
        near = is_near(j)

        @pl.when(near)
        def _():
            attend_tile(j, True, score=True)

        @pl.when(jnp.logical_not(near))
        def _():
            attend_tile(j, False, score=True)
        return carry

    def attend_only_tile(j, carry):
        near = is_near(j)

        @pl.when(near)
        def _():
            attend_tile(j, True)

        @pl.when(jnp.logical_not(near))
        def _():
            attend_tile(j, False)
        return carry

    @pl.when(scoring & attending)
    def _():
        lax.fori_loop(0, i, fused_tile, 0)

    @pl.when(jnp.logical_not(scoring))
    def _():
        lax.fori_loop(0, i, attend_only_tile, 0)

    def diagonal_score():
        for p in range(IDX_HEADS // 2):
            score_slots(i, (p,))
        score_finish(i, diagonal=True)

    def write_output():
        for h in range(N_HEADS_A):
            sl = slice(h * HEAD_DIM, (h + 1) * HEAD_DIM)
            acc = acc_ref[h]
            o = (acc[:HEAD_DIM] / acc[HEAD_DIM:HEAD_DIM + 1]).T
            g = g_ref[0, :, sl].astype(F32)
            out_ref[0, :, sl] = (o * (g / (1.0 + jnp.exp(-g)))).astype(BF16)

    when_roles(diagonal_score, write_output, score_first=False)

    if tiles_per_trip == 2:
        @pl.when(scoring & (i % 2 == 0))
        def _():
            keys_ref[cur, tile_rows(i + 1), :] = jnp.full((tk, tq), INT_MIN, I32)

    @pl.when(scoring)
    def _():
        n_kt = i + 1

        def count_where(pred):
            def body(jj, c):
                for t in range(tiles_per_trip):
                    j = jj * tiles_per_trip + t
                    hit = pred(j, keys_ref[cur, tile_rows(j), :]).astype(I32)
                    c = c + jnp.sum(hit.reshape(COUNT_CHAINS, tk // (8 * COUNT_CHAINS), 8, tq), axis=1)
                return c
            c8 = lax.fori_loop(0, (n_kt + tiles_per_trip - 1) // tiles_per_trip, body,
                               jnp.zeros((COUNT_CHAINS, 8, tq), I32))
            return jnp.sum(jnp.sum(c8, axis=0), axis=0, keepdims=True)

        key_min = jnp.min(kmin_ref[...], axis=0, keepdims=True)
        key_max = jnp.max(kmax_ref[...], axis=0, keepdims=True)

        n_causal = i * tq + lax.broadcasted_iota(I32, (1, tq), 1) + 1
        short = n_causal <= top_k
        lo0 = jnp.where(short, INT_MIN + 1, key_min)
        hi0 = jnp.where(short, INT_MIN + 2, jnp.where(key_max == INT_MAX, key_max, key_max + 1))
        cnt0 = jnp.where(short, top_k, n_causal)

        def active(lo, hi, cnt):
            return (cnt != top_k) & (hi > lo + 1)

        def any_lane(mask):
            return jnp.max(jnp.where(mask, 1.0, 0.0)) > 0.0

        def search_cond(c):
            it, lo, hi, cnt, _ = c
            return (it < SEARCH_MAX_STEPS) & any_lane(active(lo, hi, cnt))

        def search_body(c):
            it, lo, hi, cnt, cnt_hi = c
            for u in range(SEARCH_UNROLL):
                mid_key = (lo & hi) + ((lo ^ hi) >> 1)
                v_lo, v_hi = _unsortable(lo), _unsortable(hi)
                above = (cnt - top_k).astype(F32) + 0.5
                interp = above / jnp.maximum(cnt - cnt_hi, 1).astype(F32)
                frac = 0.5 + SEARCH_INTERP_WEIGHT * (interp - 0.5)
                mid_val = _sortable(v_lo + (v_hi - v_lo) * frac)
                piv = jnp.where(it + u < SEARCH_VALUE_STEPS, mid_val, mid_key)
                piv = jnp.minimum(jnp.maximum(piv, lo + 1), hi - 1)
                c_piv = count_where(lambda j, kt: kt >= piv)
                act = active(lo, hi, cnt)
                up = act & (c_piv >= top_k)
                down = act & (c_piv < top_k)
                lo, cnt = jnp.where(up, piv, lo), jnp.where(up, c_piv, cnt)
                hi, cnt_hi = jnp.where(down, piv, hi), jnp.where(down, c_piv, cnt_hi)
            return it + SEARCH_UNROLL, lo, hi, cnt, cnt_hi
        _, thr, _, cnt_thr, _ = lax.while_loop(search_cond, search_body,
                                               (jnp.int32(0), lo0, hi0, cnt0, jnp.zeros((1, tq), I32)))
        thr_ref[cur] = thr

        tied = cnt_thr > top_k

        @pl.when(any_lane(tied))
        def _():
            need = top_k - count_where(lambda j, kt: kt > thr)
            s_of = lambda j: j * tk + lax.broadcasted_iota(I32, (tk, tq), 0)

            def cut_step(n, c):
                c_lo, c_hi = c
                mid = (c_lo + c_hi) >> 1
                enough = count_where(lambda j, kt: (kt == thr) & (s_of(j) < mid)) >= need
                return jnp.where(enough, c_lo, mid), jnp.where(enough, mid, c_hi)
            _, cut = lax.fori_loop(0, max(1, (seq - 1).bit_length()), cut_step,
                                   (jnp.zeros((1, tq), I32), jnp.full((1, tq), seq, I32)))

            def demote(j, carry):
                kt = keys_ref[cur, tile_rows(j), :]
                drop = tied & (kt == thr) & (s_of(j) >= cut)
                keys_ref[cur, tile_rows(j), :] = jnp.where(drop, INT_MIN, kt)
                return carry
            lax.fori_loop(0, n_kt, demote, 0)


def _bias_table(rel_bias):
    r = jnp.arange(LANES, dtype=I32)
    relf = jnp.maximum(r, 1).astype(F32)
    large = MAX_EXACT + (jnp.log(relf / MAX_EXACT) / math.log(MAX_DISTANCE / MAX_EXACT)
                         * (N_BUCKETS - MAX_EXACT)).astype(I32)
    bucket = jnp.where(r < MAX_EXACT, r, jnp.minimum(large, N_BUCKETS - 1))
    return ((rel_bias[bucket] - rel_bias[N_BUCKETS - 1][None, :]) * LOG2E).T.astype(F32)


def _sparse_attention(proj3, kw3, positions, tbl, pmin, pmax, q_gain, k_gain, tq):
    bsz, seq, _ = proj3.shape
    top_k = min(TOPK_MAX, seq // 4)
    n_q = seq // tq
    att = lambda b, i, *_: (b, jnp.maximum(i - 1, 0))
    sco = lambda b, i, *_: (b, jnp.minimum(i, n_q - 1))
    wide = lambda seg, tile: pl.BlockSpec((1, tq, 1024), lambda *a: (*tile(*a), seg // 1024))
    full = lambda seg: pl.BlockSpec((1, seq, LANES), lambda b, i, *_: (b, 0, seg // LANES))
    fixed = lambda shape: pl.BlockSpec(shape, lambda b, i, *_: (0,) * len(shape))
    grid_spec = pltpu.PrefetchScalarGridSpec(
        num_scalar_prefetch=2,
        grid=(bsz, n_q + 1),
        in_specs=[wide(SEG_QA, att), wide(SEG_GA, att),
                  pl.BlockSpec((1, 1, tq), lambda b, i, *_: (b, 0, jnp.maximum(i - 1, 0))),
                  wide(SEG_QI, sco),
                  pl.BlockSpec((1, tq, LANES), lambda *a: (*sco(*a), 0)),
                  full(SEG_KA), full(SEG_VA), full(SEG_KIW),
                  pl.BlockSpec((1, seq, 1), lambda b, i, *_: (b, 0, 0)),
                  fixed((N_HEADS_A, LANES)), fixed((1, LANES)), fixed((1, LANES))],
        out_specs=pl.BlockSpec((1, tq, D_A), lambda *a: (*att(*a), 0)),
        scratch_shapes=[pltpu.VMEM((2, seq, tq), I32),
                        pltpu.VMEM((2, 1, tq), I32),
                        pltpu.VMEM((VT_ROWS, seq), BF16),
                        pltpu.VMEM((seq, LANES), BF16),
                        pltpu.VMEM((seq, LANES), BF16),
                        pltpu.VMEM((seq, HEAD_DIM), BF16),
                        pltpu.VMEM((N_HEADS_A, VT_ROWS, tq), F32),
                        pltpu.VMEM((N_HEADS_A, 1, tq), F32),
                        pltpu.VMEM((N_HEADS_A, 1, tq), F32),
                        pltpu.VMEM((IDX_HEADS, tq), F32),
                        pltpu.VMEM((tq, tq), F32),
                        pltpu.VMEM((tq, tq), F32),
                        pltpu.VMEM((N_HEADS_A, tq, tq), F32),
                        pltpu.VMEM((N_HEADS_A, tq, tq), BF16),
                        pltpu.VMEM((N_HEADS_A, HEAD_DIM, tq), BF16),
                        pltpu.VMEM((IDX_HEADS // 2, LANES, tq), BF16),
                        pltpu.VMEM((8, tq), I32),
                        pltpu.VMEM((8, tq), I32)])
    return pl.pallas_call(
        functools.partial(_attn_kernel, tq=tq, top_k=top_k, seq=seq),
        grid_spec=grid_spec,
        out_shape=jax.ShapeDtypeStruct((bsz, seq, D_A), BF16),
        compiler_params=pltpu.CompilerParams(dimension_semantics=("arbitrary", "arbitrary"),
                                             vmem_limit_bytes=VMEM_LIMIT),
        name="sparse_attention",
    )(pmin, pmax, proj3, proj3, positions.reshape(bsz, 1, seq), proj3, kw3, proj3, proj3, proj3,
      positions.reshape(bsz, seq, 1), tbl, q_gain.reshape(1, LANES), k_gain.reshape(1, LANES))


def _ret_kernel(gch_ref, q_ref, k_ref, v_ref, g_ref, dmask_ref, zeta_ref, xi_ref, gain_ref,
                out_ref, state_ref, intra_ref, u_ref, *, tc):
    c = pl.program_id(1)
    chunks = [slice(cc * RET_CHUNK, (cc + 1) * RET_CHUNK) for cc in range(tc // RET_CHUNK)]
    heads = [slice(h * HEAD_DIM, (h + 1) * HEAD_DIM) for h in range(N_HEADS_B)]

    @pl.when(c == 0)
    def _():
        state_ref[...] = jnp.zeros(state_ref.shape, F32)

    for h, sl in enumerate(heads):
        for cc, rows in enumerate(chunks):
            q = q_ref[0, rows, sl]
            k = k_ref[0, rows, sl]
            v = v_ref[0, rows, sl]
            s_in = _nt_dot(q, k) * dmask_ref[h]
            intra_ref[h, cc] = jnp.dot(s_in.astype(BF16), v, preferred_element_type=F32)
            kz_t = (k.astype(F32) * zeta_ref[h]).T.astype(BF16)
            u_ref[h, cc] = jnp.dot(kz_t, v, preferred_element_type=F32)

    for h, sl in enumerate(heads):
        state = state_ref[h]
        for cc, rows in enumerate(chunks):
            cross = jnp.dot(q_ref[0, rows, sl], state.astype(BF16), preferred_element_type=F32) * xi_ref[h]
            o = intra_ref[h, cc] + cross
            state = gch_ref[h] * state + u_ref[h, cc]
            mu = jnp.mean(o, axis=-1, keepdims=True)
            var = jnp.mean(jnp.square(o - mu), axis=-1, keepdims=True)
            y = (o - mu) * lax.rsqrt(var + EPS) * gain_ref[:, sl]
            g = g_ref[0, rows, sl].astype(F32)
            out_ref[0, rows, sl] = (y * (g / (1.0 + jnp.exp(-g)))).astype(BF16)
        state_ref[h] = state


def _retention_tables():
    c = RET_CHUNK
    log_g = jnp.log1p(-jnp.exp2(-5.0 - jnp.arange(N_HEADS_B, dtype=F32)))
    i = jnp.arange(c, dtype=F32)
    diff = i[:, None] - i[None, :]
    dmask = jnp.where(diff >= 0, jnp.exp(log_g[:, None, None] * jnp.maximum(diff, 0.0)), 0.0)
    zeta = jnp.exp(log_g[:, None] * (c - 1.0 - i)[None, :])
    xi = jnp.exp(log_g[:, None] * (i + 1.0)[None, :])
    g_chunk = jnp.exp(log_g * c)
    widen = lambda a: jnp.broadcast_to(a[:, :, None], (N_HEADS_B, c, HEAD_DIM))
    return g_chunk, dmask, widen(zeta), widen(xi)


def _retention(proj3, ret_gain, tables):
    bsz, seq, _ = proj3.shape
    tc = min(512, seq)
    g_chunk, dmask, zeta, xi = tables
    wide = lambda seg: pl.BlockSpec((1, tc, 1024), lambda b, c: (b, c, seg // 1024))
    table = pl.BlockSpec((N_HEADS_B, RET_CHUNK, HEAD_DIM), lambda b, c: (0, 0, 0))
    return pl.pallas_call(
        functools.partial(_ret_kernel, tc=tc),
        grid=(bsz, seq // tc),
        in_specs=[pl.BlockSpec(memory_space=pltpu.SMEM),
                  wide(SEG_QB), wide(SEG_KB), wide(SEG_VB), wide(SEG_GB),
                  table, table, table,
                  pl.BlockSpec((1, D_B), lambda b, c: (0, 0))],
        out_specs=pl.BlockSpec((1, tc, D_B), lambda b, c: (b, c, 0)),
        out_shape=jax.ShapeDtypeStruct((bsz, seq, D_B), BF16),
        scratch_shapes=[pltpu.VMEM((N_HEADS_B, HEAD_DIM, HEAD_DIM), F32),
                        pltpu.VMEM((N_HEADS_B, tc // RET_CHUNK, RET_CHUNK, HEAD_DIM), F32),
                        pltpu.VMEM((N_HEADS_B, tc // RET_CHUNK, HEAD_DIM, HEAD_DIM), F32)],
        compiler_params=pltpu.CompilerParams(dimension_semantics=("arbitrary", "arbitrary"),
                                             vmem_limit_bytes=VMEM_LIMIT),
        name="retention",
    )(g_chunk, proj3, proj3, proj3, proj3, dmask, zeta, xi, ret_gain.reshape(1, D_B))


def _out_kernel(ya_ref, yb_ref, wa_ref, wb_ref, x_ref, gate_ref, out_ref):
    y = jnp.dot(ya_ref[...], wa_ref[0], preferred_element_type=F32)
    y = y + jnp.dot(yb_ref[...], wb_ref[0], preferred_element_type=F32)
    out_ref[...] = x_ref[...] + gate_ref[0] * y


def _out_projection(ya2, yb2, w_out, layer, x2, gate, seq):
    m, d = x2.shape
    tm = min(512, seq)
    tn = d
    per_batch = seq // tm
    bsz = gate.shape[0]
    return pl.pallas_call(
        _out_kernel,
        grid=(m // tm, d // tn),
        in_specs=[pl.BlockSpec((tm, D_A), lambda i, j: (i, 0)),
                  pl.BlockSpec((tm, D_B), lambda i, j: (i, 0)),
                  pl.BlockSpec((1, D_A, tn), lambda i, j: (layer, 0, j)),
                  pl.BlockSpec((1, D_B, tn), lambda i, j: (layer, 1, j)),
                  pl.BlockSpec((tm, tn), lambda i, j: (i, j)),
                  pl.BlockSpec((1, 1, tn), lambda i, j: (i // per_batch, 0, j))],
        out_specs=pl.BlockSpec((tm, tn), lambda i, j: (i, j)),
        out_shape=jax.ShapeDtypeStruct((m, d), F32),
        compiler_params=pltpu.CompilerParams(dimension_semantics=("arbitrary", "arbitrary"),
                                             vmem_limit_bytes=VMEM_LIMIT),
        name="out_proj_residual",
    )(ya2, yb2, w_out, w_out, x2, gate.reshape(bsz, 1, d))


def kernel(x, c, positions, rel_bias, norm_gain, w_mod, b_mod, w_in, q_norm_gain, k_norm_gain,
           ret_norm_gain, w_out):
    bsz, seq, d = x.shape
    depth = w_in.shape[0]
    m = bsz * seq
    tq = min(256, seq)
    assert d == D_A + D_B and seq % tq == 0 and seq % RET_CHUNK == 0

    mod = _modulation(c, w_mod, b_mod)
    cs = _rope_tables(positions)
    tbl = _bias_table(rel_bias)
    pos_tiles = positions.reshape(bsz, seq // tq, tq)
    pmin = jnp.min(pos_tiles, axis=-1).astype(I32)
    pmax = jnp.max(pos_tiles, axis=-1).astype(I32)
    tables = _retention_tables()
    w_in_packed = _pack_w_in(w_in)
    w_out_bf16 = w_out.astype(BF16)

    x2 = x.reshape(m, d)
    for l in range(depth):
        shift, scale, gate = mod[l, :, :d], mod[l, :, d:2 * d], mod[l, :, 2 * d:]
        proj, kw = _in_projection(x2, norm_gain[l], scale, shift, w_in_packed, l, cs, seq)
        proj3 = proj.reshape(bsz, seq, N_PACK)
        y_a = _sparse_attention(proj3, kw.reshape(bsz, seq, LANES), positions, tbl, pmin, pmax,
                                q_norm_gain[l], k_norm_gain[l], tq)
        y_b = _retention(proj3, ret_norm_gain[l], tables)
        x2 = _out_projection(y_a.reshape(m, D_A), y_b.reshape(m, D_B), w_out_bf16, l, x2, gate, seq)
    return x2.reshape(bsz, seq, d)
```
